```python
import jax, jax.numpy as jnp
from jax import lax
import numpy as np

D_MODEL = 1024
BATCH = 8
SEQ = 4096
DEPTH = 1

D_MIX = D_MODEL
W_A = D_MIX // 2
HA_HEAD_DIM = 128
HA_HEADS = W_A // HA_HEAD_DIM
W_B = D_MIX - W_A
HB_HEAD_DIM = 64
HB_HEADS = W_B // HB_HEAD_DIM
HGRN_CHUNK = 64
DECAY_LORA = max(32, int(round(1.8 * W_B ** 0.5 / 32)) * 32)
AAA_LORA = max(32, int(round(1.8 * W_B ** 0.5 / 32)) * 32)
GATE_LORA = max(32, int(round(0.6 * W_B ** 0.8 / 32)) * 32)
N_HGRN_COLS = 4 * W_A
N_RWKV_COLS = 3 * W_B + DECAY_LORA + AAA_LORA + GATE_LORA
D_IN_PROJ = N_HGRN_COLS + N_RWKV_COLS
D_FF = ((8 * D_MODEL // 3 + 255) // 256) * 256
NORM_EPS = 1e-6
RWKV_GN_EPS = 64e-5
L2_EPS = 1e-12

kernel_name = 'hybrid_hgrn2_rwkv7_macaron'


def rms_norm(x, g):
    xf = x.astype(jnp.float32)
    y = xf * lax.rsqrt(jnp.mean(xf * xf, axis=-1, keepdims=True) + NORM_EPS)
    return (y * g).astype(x.dtype)


def swiglu(h, w_gate, w_up, w_down):
    return (jax.nn.silu(h @ w_gate) * (h @ w_up)) @ w_down


def token_shift(t):
    return jnp.pad(t, ((0, 0), (1, 0), (0, 0)))[:, :-1]


def hgrn2_chunkwise(q, k, v, log_f):
    B, T, H, K = q.shape
    V = v.shape[-1]
    C = HGRN_CHUNK
    n = T // C

    def blocks(t):
        return t.reshape(B, n, C, H, t.shape[-1]).transpose(0, 3, 1, 2, 4)

    q, k, v, log_f = blocks(q), blocks(k), blocks(v), blocks(log_f)
    b = jnp.cumsum(log_f, axis=3)
    b_ref = b[:, :, :, C // 2:C // 2 + 1]
    b_last = b[:, :, :, C - 1:]
    scores = jnp.einsum('bhntk,bhnsk->bhnts', q * jnp.exp(b - b_ref), k * jnp.exp(b_ref - b))
    causal = jnp.tril(jnp.ones((C, C), dtype=bool))
    scores = jnp.where(causal, scores, 0.0)
    o = jnp.einsum('bhnts,bhnsv->bhntv', scores, v)
    u = jnp.einsum('bhnsk,bhnsv->bhnkv', k * jnp.exp(b_last - b), v)
    d = jnp.exp(b_last[:, :, :, 0])

    def chunk_step(s, inp):
        u_n, d_n = inp
        return d_n[..., None] * s + u_n, s

    _, s_prev = lax.scan(chunk_step, jnp.zeros((B, H, K, V), jnp.float32),
                         (jnp.moveaxis(u, 2, 0), jnp.moveaxis(d, 2, 0)))
    s_prev = jnp.moveaxis(s_prev, 0, 2)
    o = o + jnp.einsum('bhntk,bhnkv->bhntv', q * jnp.exp(b), s_prev)
    return o.transpose(0, 2, 3, 1, 4).reshape(B, T, H, V)


def rwkv7_scan(r, w, k, v, a, b):
    B, T, H, N = r.shape

    def step(s, inp):
        r_t, w_t, k_t, v_t, a_t, b_t = inp
        sa = jnp.einsum('bhvk,bhk->bhv', s, a_t)
        s = s * w_t[:, :, None, :] + sa[..., None] * b_t[:, :, None, :] + v_t[..., None] * k_t[:, :, None, :]
        return s, jnp.einsum('bhvk,bhk->bhv', s, r_t)

    tm = lambda t: jnp.moveaxis(t, 1, 0)
    _, y = lax.scan(step, jnp.zeros((B, H, N, N), jnp.float32),
                    (tm(r), tm(w), tm(k), tm(v), tm(a), tm(b)))
    return jnp.moveaxis(y, 0, 1)


def hybrid_mixer(h, w_in, lb, hgrn_out_norm, mu, w0, w2, a0, a2, g2, k_k, k_a, r_k, gn_w, gn_b, w_out):
    B, T, _ = h.shape
    f32 = jnp.float32
    p = h @ w_in

    q_a, f_a, i_a, g_a = jnp.split(p[..., :N_HGRN_COLS].astype(f32), 4, axis=-1)
    forget = lb + (1.0 - lb) * jax.nn.sigmoid(f_a)
    heads_a = lambda t: t.reshape(B, T, HA_HEADS, HA_HEAD_DIM)
    o_a = hgrn2_chunkwise(heads_a(jax.nn.silu(q_a)), heads_a(1.0 - forget),
                          heads_a(i_a), heads_a(jnp.log(forget)))
    o_a = o_a * lax.rsqrt(jnp.mean(o_a * o_a, axis=-1, keepdims=True) + NORM_EPS)
    o_a = o_a.reshape(B, T, W_A) * hgrn_out_norm * jax.nn.silu(g_a)

    pr = p[..., N_HGRN_COLS:].astype(f32)
    pr = pr + mu * (token_shift(pr) - pr)
    splits = [W_B, 2 * W_B, 3 * W_B, 3 * W_B + DECAY_LORA, 3 * W_B + DECAY_LORA + AAA_LORA]
    r, k, v, w_low, a_low, g_low = jnp.split(pr, splits, axis=-1)
    w_log = -jax.nn.softplus(-(w0 + jnp.tanh(w_low) @ w2)) - 0.5
    decay = jnp.exp(-jnp.exp(w_log))
    a = jax.nn.sigmoid(a0 + a_low @ a2)
    g = jax.nn.sigmoid(g_low) @ g2
    heads_b = lambda t: t.reshape(B, T, HB_HEADS, HB_HEAD_DIM)
    kk = heads_b(k * k_k)
    kk = kk / jnp.maximum(jnp.sqrt(jnp.sum(kk * kk, axis=-1, keepdims=True)), L2_EPS)
    k = k * (1.0 + (a - 1.0) * k_a)
    r_h, k_h, v_h, a_h = heads_b(r), heads_b(k), heads_b(v), heads_b(a)
    y = rwkv7_scan(r_h, heads_b(decay), k_h, v_h, -kk, kk * a_h)
    mean = jnp.mean(y, axis=-1, keepdims=True)
    var = jnp.mean(jnp.square(y - mean), axis=-1, keepdims=True)
    y = ((y - mean) * lax.rsqrt(var + RWKV_GN_EPS)).reshape(B, T, W_B) * gn_w + gn_b
    bonus = (jnp.sum(r_h * k_h * r_k, axis=-1, keepdims=True) * v_h).reshape(B, T, W_B)
    o_b = (y + bonus) * g

    return jnp.concatenate([o_a, o_b], axis=-1).astype(h.dtype) @ w_out


def setup_inputs(seed: int = 0) -> dict:
    key = jax.random.key(seed)
    ks = iter(jax.random.split(key, 40))
    f32 = jnp.float32
    nrm = lambda shape, scale: jax.random.normal(next(ks), shape, f32) * scale
    uni = lambda shape, lo, hi: jax.random.uniform(next(ks), shape, f32, lo, hi)
    L = DEPTH
    return {
        'x': nrm((BATCH, SEQ, D_MODEL), 1.0),
        'ffn1_norm': 1.0 + nrm((L, D_MODEL), 0.02),
        'ffn1_w_gate': nrm((L, D_MODEL, D_FF), D_MODEL ** -0.5),
        'ffn1_w_up': nrm((L, D_MODEL, D_FF), D_MODEL ** -0.5),
        'ffn1_w_down': nrm((L, D_FF, D_MODEL), D_FF ** -0.5),
        'mix_norm': 1.0 + nrm((L, D_MODEL), 0.02),
        'w_in': nrm((L, D_MODEL, D_IN_PROJ), D_MODEL ** -0.5),
        'hgrn_lb_logits': nrm((L + 1, W_A), 0.1),
        'hgrn_out_norm': 1.0 + nrm((L, W_A), 0.02),
        'rwkv_shift_mu': uni((L, N_RWKV_COLS), 0.0, 1.0),
        'rwkv_w0': uni((L, W_B), -5.0, 1.0),
        'rwkv_w2': nrm((L, DECAY_LORA, W_B), 0.5 * DECAY_LORA ** -0.5),
        'rwkv_a0': nrm((L, W_B), 0.1),
        'rwkv_a2': nrm((L, AAA_LORA, W_B), 0.5 * AAA_LORA ** -0.5),
        'rwkv_g2': nrm((L, GATE_LORA, W_B), GATE_LORA ** -0.5),
        'rwkv_k_k': 0.85 + nrm((L, W_B), 0.05),
        'rwkv_k_a': 1.0 + nrm((L, W_B), 0.05),
        'rwkv_r_k': nrm((L, HB_HEADS, HB_HEAD_DIM), 0.1),
        'rwkv_gn_w': 1.0 + nrm((L, W_B), 0.02),
        'rwkv_gn_b': nrm((L, W_B), 0.02),
        'w_out': nrm((L, D_MIX, D_MODEL), D_MIX ** -0.5),
        'ffn2_norm': 1.0 + nrm((L, D_MODEL), 0.02),
        'ffn2_w_gate': nrm((L, D_MODEL, D_FF), D_MODEL ** -0.5),
        'ffn2_w_up': nrm((L, D_MODEL, D_FF), D_MODEL ** -0.5),
        'ffn2_w_down': nrm((L, D_FF, D_MODEL), D_FF ** -0.5),
        'final_norm': 1.0 + nrm((D_MODEL,), 0.02),
    }


def reference(x, ffn1_norm, ffn1_w_gate, ffn1_w_up, ffn1_w_down, mix_norm, w_in, hgrn_lb_logits,
              hgrn_out_norm, rwkv_shift_mu, rwkv_w0, rwkv_w2, rwkv_a0, rwkv_a2, rwkv_g2, rwkv_k_k,
              rwkv_k_a, rwkv_r_k, rwkv_gn_w, rwkv_gn_b, w_out, ffn2_norm, ffn2_w_gate, ffn2_w_up,
              ffn2_w_down, final_norm):
    lower_bounds = jnp.cumsum(jax.nn.softmax(hgrn_lb_logits.astype(jnp.float32), axis=0), axis=0)
    for l in range(DEPTH):
        x = x + 0.5 * swiglu(rms_norm(x, ffn1_norm[l]), ffn1_w_gate[l], ffn1_w_up[l], ffn1_w_down[l])
        x = x + hybrid_mixer(rms_norm(x, mix_norm[l]), w_in[l], lower_bounds[l], hgrn_out_norm[l],
                             rwkv_shift_mu[l], rwkv_w0[l], rwkv_w2[l], rwkv_a0[l], rwkv_a2[l],
                             rwkv_g2[l], rwkv_k_k[l], rwkv_k_a[l], rwkv_r_k[l], rwkv_gn_w[l],
                             rwkv_gn_b[l], w_out[l])
        x = x + 0.5 * swiglu(rms_norm(x, ffn2_norm[l]), ffn2_w_gate[l], ffn2_w_up[l], ffn2_w_down[l])
    return rms_norm(x, final_norm)
```

```python
import functools

import jax
import jax.numpy as jnp
from jax import lax
from jax.experimental import pallas as pl
from jax.experimental.pallas import tpu as pltpu

F32 = jnp.float32
BF16 = jnp.bfloat16

NORM_EPS = 1e-6
RWKV_GN_EPS = 64e-5
L2_EPS = 1e-12

LANES = 128
MXU_COLS = 256
HA_HEAD_DIM = 128
HB_HEAD_DIM = 64
CHUNK = 64
DECAY_LORA, AAA_LORA, GATE_LORA = 32, 32, 96
LORA_PAD = 256
FFN_ROW_TILE = 512
FFN_COL_TILE = 768
MIXER_BATCH_TILE = 2
MIB = 1024 * 1024


def _dot(a, b):
    return jnp.dot(a, b, preferred_element_type=F32)


def _dot_nt(a, b):
    return lax.dot_general(a, b, (((1,), (1,)), ((), ())), preferred_element_type=F32)


def _dot_tn(a, b):
    return lax.dot_general(a, b, (((0,), (0,)), ((), ())), preferred_element_type=F32)


def _split_bf16(x):
    hi = x.astype(BF16)
    lo = (x - hi.astype(F32)).astype(BF16)
    return hi, lo


def _rms_norm(x, g):
    return x * lax.rsqrt(jnp.mean(x * x, axis=-1, keepdims=True) + NORM_EPS) * g


def _sigmoid(x):
    return 1.0 / (1.0 + jnp.exp(-x))


def _softplus(x):
    return jnp.maximum(x, 0.0) + jnp.log1p(jnp.exp(-jnp.abs(x)))


def _ffn_kernel(x_ref, g_ref, wg_ref, wu_ref, wd_ref, fn_ref, o_ref, *, col_tiles, final_norm):
    x = x_ref[...]
    h = _rms_norm(x, g_ref[...]).astype(BF16)
    acc = jnp.zeros(x.shape, F32)
    for lo, hi in col_tiles:
        gate = _dot(h, wg_ref[:, lo:hi])
        up = _dot(h, wu_ref[:, lo:hi])
        act = (gate * _sigmoid(gate) * up).astype(BF16)
        acc = acc + _dot(act, wd_ref[lo:hi, :])
    y = x + 0.5 * acc
    if final_norm:
        y = _rms_norm(y, fn_ref[...])
    o_ref[...] = y


def _resident(shape):
    return pl.BlockSpec(shape, lambda *_: (0,) * len(shape), pipeline_mode=pl.Buffered(1))


def _ffn(x2d, norm_g, w_gate, w_up, w_down, final_g, *, final_norm):
    n, d = x2d.shape
    d_ff = w_gate.shape[1]
    tm = FFN_ROW_TILE
    assert n % tm == 0 and d_ff % MXU_COLS == 0
    col_tiles = tuple((lo, min(lo + FFN_COL_TILE, d_ff)) for lo in range(0, d_ff, FFN_COL_TILE))
    weight_bytes = 3 * d * d_ff * 2
    tile_bytes = tm * d * 4
    vmem = weight_bytes + 8 * tile_bytes + 3 * tm * FFN_COL_TILE * 4 + 8 * MIB
    return pl.pallas_call(
        functools.partial(_ffn_kernel, col_tiles=col_tiles, final_norm=final_norm),
        grid=(n // tm,),
        in_specs=[
            pl.BlockSpec((tm, d), lambda i: (i, 0)),
            _resident((1, d)),
            _resident((d, d_ff)),
            _resident((d, d_ff)),
            _resident((d_ff, d)),
            _resident((1, d)),
        ],
        out_specs=pl.BlockSpec((tm, d), lambda i: (i, 0)),
        out_shape=jax.ShapeDtypeStruct((n, d), F32),
        compiler_params=pltpu.CompilerParams(
            dimension_semantics=("arbitrary",), vmem_limit_bytes=vmem),
        name="ffn_final" if final_norm else "ffn",
    )(x2d, norm_g, w_gate, w_up, w_down, final_g)


def _iota2(shape, axis):
    return lax.broadcasted_iota(jnp.int32, shape, axis)


def _cumsum_rows(tri, x):
    hi, lo = _split_bf16(x)
    return _dot(tri, hi) + _dot(tri, lo)


def _segsum(x, ones2):
    hi, lo = _split_bf16(x)
    outs = []
    for j in range(x.shape[1] // LANES):
        sl = slice(j * LANES, (j + 1) * LANES)
        outs.append(_dot(hi[:, sl], ones2) + _dot(lo[:, sl], ones2))
    return jnp.concatenate(outs, axis=1)


def _hgrn_batch(p_ref, rows, lb, tri, causal, hs_ref, b):
    c = CHUNK
    w_a = lb.shape[1]
    q_a = p_ref[rows, 0:w_a]
    f_a = p_ref[rows, w_a:2 * w_a]
    v = p_ref[rows, 2 * w_a:3 * w_a].astype(BF16)
    forget = lb + (1.0 - lb) * _sigmoid(f_a)
    log_f = jnp.log(forget)
    kh = 1.0 - forget
    q = q_a * _sigmoid(q_a)
    bc = _cumsum_rows(tri, log_f)
    b_ref = bc[c // 2:c // 2 + 1]
    b_last = bc[c - 1:c]
    qt = (q * jnp.exp(bc - b_ref)).astype(BF16)
    kt = (kh * jnp.exp(b_ref - bc)).astype(BF16)
    kl = (kh * jnp.exp(b_last - bc)).astype(BF16)
    q0 = (q * jnp.exp(bc)).astype(BF16)
    d = jnp.exp(b_last)
    outs = []
    for h in range(w_a // HA_HEAD_DIM):
        sl = slice(h * HA_HEAD_DIM, (h + 1) * HA_HEAD_DIM)
        s_t = hs_ref[b, h]
        sc = jnp.where(causal, _dot_nt(qt[:, sl], kt[:, sl]), 0.0).astype(BF16)
        o = _dot(sc, v[:, sl]) + _dot_nt(q0[:, sl], s_t.astype(BF16))
        hs_ref[b, h] = d[:, sl] * s_t + _dot_tn(v[:, sl], kl[:, sl])
        outs.append(o * lax.rsqrt(jnp.mean(o * o, axis=-1, keepdims=True) + NORM_EPS))
    return jnp.concatenate(outs, axis=1)


def _rwkv_pair(ops, s_ref, b, pr, gmask, bdmask, lane_lo):
    c = CHUNK
    at, rt, bt, kt, a0, r0, bl, kl, v, dl = ops
    s_old = s_ref[b, pr]
    z = _dot_nt(jnp.concatenate([a0, r0], axis=0), s_old.astype(BF16))
    ar = jnp.concatenate([at, rt], axis=0)
    bk = jnp.concatenate([bt, kt], axis=0)
    zero = jnp.zeros_like(v)
    y = z[c:]
    u_pair = None
    for hh in range(2):
        mh = lane_lo if hh == 0 else jnp.logical_not(lane_lo)
        v_h = jnp.where(mh, v, zero)
        g = jnp.where(gmask, _dot_nt(jnp.where(mh, ar, jnp.zeros_like(ar)), bk), 0.0)
        g_top = g[:c]
        lak = jnp.where(lane_lo, 0.0, g_top).astype(BF16)
        x = jnp.where(mh, z[:c], 0.0) + _dot(lak, jnp.concatenate([zero, v_h], axis=0))
        pw = g_top[:, :c].astype(BF16)
        n = 1
        while n < c:
            x = x + _dot(pw, x.astype(BF16))
            n *= 2
            if n < c:
                pw = _dot(pw, pw).astype(BF16)
        u_pair = x if u_pair is None else u_pair + x
        y = y + _dot(g[c:].astype(BF16), jnp.concatenate([x.astype(BF16), v_h], axis=0))
    uv = jnp.concatenate([u_pair.astype(BF16), v], axis=0)
    upd = _dot_tn(uv, jnp.concatenate([bl, kl], axis=0))
    s_ref[b, pr] = dl * s_old + jnp.where(bdmask, upd, 0.0)
    return y


def _mixer_kernel(x_ref, mn_ref, win_ref, lbl_ref, hon_ref, mu_ref, w0_ref, a0_ref, wcomb_ref,
                  kk_ref, ka_ref, rk_ref, gnw_ref, gnb_ref, wout_ref, o_ref,
                  p_ref, hs_ref, rs_ref, carry_ref, *, nb, w_a, w_b):
    c = CHUNK
    d_model = x_ref.shape[-1]
    n_rkv = 3 * w_b + LORA_PAD

    @pl.when(pl.program_id(1) == 0)
    def _():
        hs_ref[...] = jnp.zeros_like(hs_ref)
        rs_ref[...] = jnp.zeros_like(rs_ref)
        carry_ref[...] = jnp.zeros_like(carry_ref)

    x = x_ref[...].reshape(nb * c, d_model)
    h = _rms_norm(x, mn_ref[...]).astype(BF16)
    p_ref[...] = _dot(h, win_ref[...])

    logits = lbl_ref[...]
    e = jnp.exp(logits - jnp.max(logits, axis=0, keepdims=True))
    lb = e[0:1] / jnp.sum(e, axis=0, keepdims=True)

    ri, ci = _iota2((c, c), 0), _iota2((c, c), 1)
    causal = ci <= ri
    tri = causal.astype(BF16)
    gi, gj = _iota2((2 * c, 2 * c), 0), _iota2((2 * c, 2 * c), 1)
    ti, sj = gi & (c - 1), gj & (c - 1)
    gmask = (sj < ti) | ((gi >= c) & (sj == ti))
    li, lj = _iota2((LANES, LANES), 0), _iota2((LANES, LANES), 1)
    bdmask = lax.shift_right_logical(li, 6) == lax.shift_right_logical(lj, 6)
    ones2 = bdmask.astype(BF16)
    lane_lo = _iota2((1, LANES), 1) < HB_HEAD_DIM
    row0 = _iota2((c, 1), 0) == 0
    ll = _iota2((1, LORA_PAD), 1)

    mu = mu_ref[...]
    outs = []
    for b in range(nb):
        rows = slice(b * c, (b + 1) * c)
        o_a = _hgrn_batch(p_ref, rows, lb, tri, causal, hs_ref, b)
        g_a = p_ref[rows, 3 * w_a:4 * w_a]
        o_a = o_a * hon_ref[...] * (g_a * _sigmoid(g_a))

        pb = p_ref[rows, 4 * w_a:4 * w_a + n_rkv]
        shifted = jnp.where(row0, carry_ref[b:b + 1, :], pltpu.roll(pb, 1, 0))
        carry_ref[b:b + 1, :] = pb[c - 1:c, :]
        pb = pb + mu * (shifted - pb)
        r = pb[:, 0:w_b]
        k = pb[:, w_b:2 * w_b]
        v = pb[:, 2 * w_b:3 * w_b]
        low = pb[:, 3 * w_b:]
        act = jnp.where(ll < DECAY_LORA, jnp.tanh(low),
                        jnp.where(ll < DECAY_LORA + AAA_LORA, low,
                                  jnp.where(ll < DECAY_LORA + AAA_LORA + GATE_LORA, _sigmoid(low), 0.0)))
        ld = _dot(act.astype(BF16), wcomb_ref[...])
        w_log = -_softplus(-(w0_ref[...] + ld[:, 0:w_b])) - 0.5
        lw = -jnp.exp(w_log)
        a = _sigmoid(a0_ref[...] + ld[:, w_b:2 * w_b])
        gate = ld[:, 2 * w_b:3 * w_b]
        kk = k * kk_ref[...]
        kk = kk / jnp.maximum(jnp.sqrt(_segsum(kk * kk, ones2)), L2_EPS)
        k = k * (1.0 + (a - 1.0) * ka_ref[...])
        bonus = _segsum(r * k * rk_ref[...], ones2) * v
        av, bv = -kk, kk * a

        gc = _cumsum_rows(tri, lw)
        gprev = gc - lw
        gref = gc[c // 2:c // 2 + 1]
        glast = gc[c - 1:c]
        e_dn = jnp.exp(gref - gc)
        e_last = jnp.exp(glast - gc)
        full = (
            av * jnp.exp(gprev - gref), r * jnp.exp(gc - gref), bv * e_dn, k * e_dn,
            av * jnp.exp(gprev), r * jnp.exp(gc), bv * e_last, k * e_last, v)
        full = tuple(t.astype(BF16) for t in full)
        dl = jnp.exp(glast)
        ys = []
        for pr in range(w_b // LANES):
            sl = slice(pr * LANES, (pr + 1) * LANES)
            ops = tuple(t[:, sl] for t in full) + (dl[:, sl],)
            ys.append(_rwkv_pair(ops, rs_ref, b, pr, gmask, bdmask, lane_lo))
        y = jnp.concatenate(ys, axis=1)

        inv_n = 1.0 / HB_HEAD_DIM
        yc = y - _segsum(y, ones2) * inv_n
        var = _segsum(yc * yc, ones2) * inv_n
        yn = yc * lax.rsqrt(var + RWKV_GN_EPS) * gnw_ref[...] + gnb_ref[...]
        o_b = (yn + bonus) * gate
        outs.append(jnp.concatenate([o_a, o_b], axis=1).astype(BF16))

    o = jnp.concatenate(outs, axis=0)
    o_ref[...] = (x + _dot(o, wout_ref[...])).reshape(nb, c, d_model)


def _mixer(x, mix_norm, w_in, lb_logits, hgrn_out_norm, mu, w0, w2, a0, a2, g2, k_k, k_a, r_k,
           gn_w, gn_b, w_out):
    bsz, t, d = x.shape
    w_a = hgrn_out_norm.shape[-1]
    w_b = w0.shape[-1]
    nb, c = MIXER_BATCH_TILE, CHUNK
    n_lora = DECAY_LORA + AAA_LORA + GATE_LORA
    assert bsz % nb == 0 and t % c == 0
    assert w_a % HA_HEAD_DIM == 0 and w_b % LANES == 0
    assert w_in.shape[1] == 4 * w_a + 3 * w_b + n_lora
    pad = LORA_PAD - n_lora
    n_cols = 4 * w_a + 3 * w_b + LORA_PAD
    n_rkv = 3 * w_b + LORA_PAD
    w_in_p = jnp.pad(w_in, ((0, 0), (0, pad))).astype(BF16)
    mu_p = jnp.pad(mu.reshape(1, -1), ((0, 0), (0, pad)))
    wcomb = jnp.zeros((LORA_PAD, 3 * w_b), F32)
    wcomb = wcomb.at[0:DECAY_LORA, 0:w_b].set(w2)
    wcomb = wcomb.at[DECAY_LORA:DECAY_LORA + AAA_LORA, w_b:2 * w_b].set(a2)
    wcomb = wcomb.at[DECAY_LORA + AAA_LORA:n_lora, 2 * w_b:3 * w_b].set(g2)
    row = lambda vec: vec.reshape(1, -1).astype(F32)
    n_pairs = w_b // LANES
    n_heads_a = w_a // HA_HEAD_DIM
    state_bytes = nb * (n_heads_a * HA_HEAD_DIM * HA_HEAD_DIM + n_pairs * LANES * LANES) * 4
    weight_bytes = (d * n_cols + LORA_PAD * 3 * w_b + (w_a + w_b) * d) * 2
    vmem = weight_bytes + state_bytes + 3 * nb * c * n_cols * 4 + 4 * nb * c * d * 4 + 12 * MIB
    return pl.pallas_call(
        functools.partial(_mixer_kernel, nb=nb, w_a=w_a, w_b=w_b),
        grid=(bsz // nb, t // c),
        in_specs=[
            pl.BlockSpec((nb, c, d), lambda i, j: (i, j, 0)),
            _resident((1, d)),
            _resident((d, n_cols)),
            _resident(lb_logits.shape),
            _resident((1, w_a)),
            _resident((1, n_rkv)),
            _resident((1, w_b)),
            _resident((1, w_b)),
            _resident((LORA_PAD, 3 * w_b)),
            _resident((1, w_b)),
            _resident((1, w_b)),
            _resident((1, w_b)),
            _resident((1, w_b)),
            _resident((1, w_b)),
            _resident((w_a + w_b, d)),
        ],
        out_specs=pl.BlockSpec((nb, c, d), lambda i, j: (i, j, 0)),
        out_shape=jax.ShapeDtypeStruct((bsz, t, d), F32),
        scratch_shapes=[
            pltpu.VMEM((nb * c, n_cols), F32),
            pltpu.VMEM((nb, n_heads_a, HA_HEAD_DIM, HA_HEAD_DIM), F32),
            pltpu.VMEM((nb, n_pairs, LANES, LANES), F32),
            pltpu.VMEM((nb, n_rkv), F32),
        ],
        compiler_params=pltpu.CompilerParams(
            dimension_semantics=("arbitrary", "arbitrary"), vmem_limit_bytes=vmem),
        name="mixer",
    )(x, row(mix_norm), w_in_p, lb_logits.astype(F32), row(hgrn_out_norm), mu_p, row(w0), row(a0),
      wcomb.astype(BF16), row(k_k), row(k_a), row(r_k), row(gn_w), row(gn_b), w_out.astype(BF16))


def kernel(x, ffn1_norm, ffn1_w_gate, ffn1_w_up, ffn1_w_down, mix_norm, w_in, hgrn_lb_logits, hgrn_out_norm, rwkv_shift_mu, rwkv_w0, rwkv_w2, rwkv_a0, rwkv_a2, rwkv_g2, rwkv_k_k, rwkv_k_a, rwkv_r_k, rwkv_gn_w, rwkv_gn_b, w_out, ffn2_norm, ffn2_w_gate, ffn2_w_up, ffn2_w_down, final_norm):
    bsz, t, d = x.shape
    depth = ffn1_norm.shape[0]
    assert depth == 1 and hgrn_lb_logits.shape[0] == depth + 1
    l = 0
    row = lambda vec: vec.reshape(1, -1).astype(F32)
    fg = row(final_norm)
    h = _ffn(x.reshape(bsz * t, d), row(ffn1_norm[l]), ffn1_w_gate[l].astype(BF16),
             ffn1_w_up[l].astype(BF16), ffn1_w_down[l].astype(BF16), fg, final_norm=False)
    h = _mixer(h.reshape(bsz, t, d), mix_norm[l], w_in[l], hgrn_lb_logits, hgrn_out_norm[l],
               rwkv_shift_mu[l], rwkv_w0[l], rwkv_w2[l], rwkv_a0[l], rwkv_a2[l], rwkv_g2[l],
               rwkv_k_k[l], rwkv_k_a[l], rwkv_r_k[l], rwkv_gn_w[l], rwkv_gn_b[l], w_out[l])
    h = _ffn(h.reshape(bsz * t, d), row(ffn2_norm[l]), ffn2_w_gate[l].astype(BF16),
             ffn2_w_up[l].astype(BF16), ffn2_w_down[l].astype(BF16), fg, final_norm=True)
    return h.reshape(bsz, t, d)
```

```python
import functools

import jax
import jax.numpy as jnp
from jax import lax
from jax.experimental import pallas as pl
from jax.experimental.pallas import tpu as pltpu

F32 = jnp.float32
BF16 = jnp.bfloat16

NORM_EPS = 1e-6
RWKV_GN_EPS = 64e-5
L2_EPS = 1e-12

LANES = 128
MXU_COLS = 256
HA_HEAD_DIM = 128
HB_HEAD_DIM = 64
CHUNK = 64
DECAY_LORA, AAA_LORA, GATE_LORA = 32, 32, 96
LORA_PAD = 256
FFN_ROW_TILE = 512
FFN_COL_TILE = 768
MIXER_BATCH_TILE = 2
MIB = 1024 * 1024


def _dot(a, b):
    return jnp.dot(a, b, preferred_element_type=F32)


def _dot_nt(a, b):
    return lax.dot_general(a, b, (((1,), (1,)), ((), ())), preferred_element_type=F32)


def _dot_tn(a, b):
    return lax.dot_general(a, b, (((0,), (0,)), ((), ())), preferred_element_type=F32)


def _split_bf16(x):
    hi = x.astype(BF16)
    lo = (x - hi.astype(F32)).astype(BF16)
    return hi, lo


def _rms_norm(x, g):
    return x * lax.rsqrt(jnp.mean(x * x, axis=-1, keepdims=True) + NORM_EPS) * g


def _sigmoid(x):
    return 1.0 / (1.0 + jnp.exp(-x))


def _softplus(x):
    return jnp.maximum(x, 0.0) + jnp.log1p(jnp.exp(-jnp.abs(x)))


def _ffn_kernel(x_ref, g_ref, wg_ref, wu_ref, wd_ref, fn_ref, o_ref, *, col_tiles, final_norm):
    x = x_ref[...]
    h = _rms_norm(x, g_ref[...]).astype(BF16)
    acc = jnp.zeros(x.shape, F32)
    for lo, hi in col_tiles:
        gate = _dot(h, wg_ref[:, lo:hi])
        up = _dot(h, wu_ref[:, lo:hi])
        act = (gate * _sigmoid(gate) * up).astype(BF16)
        acc = acc + _dot(act, wd_ref[lo:hi, :])
    y = x + 0.5 * acc
    if final_norm:
        y = _rms_norm(y, fn_ref[...])
    o_ref[...] = y


def _resident(shape):
    return pl.BlockSpec(shape, lambda *_: (0,) * len(shape), pipeline_mode=pl.Buffered(1))


def _ffn(x2d, norm_g, w_gate, w_up, w_down, final_g, *, final_norm):
    n, d = x2d.shape
    d_ff = w_gate.shape[1]
    tm = FFN_ROW_TILE
    assert n % tm == 0 and d_ff % MXU_COLS == 0
    col_tiles = tuple((lo, min(lo + FFN_COL_TILE, d_ff)) for lo in range(0, d_ff, FFN_COL_TILE))
    weight_bytes = 3 * d * d_ff * 2
    tile_bytes = tm * d * 4
    vmem = weight_bytes + 8 * tile_bytes + 3 * tm * FFN_COL_TILE * 4 + 8 * MIB
    return pl.pallas_call(
        functools.partial(_ffn_kernel, col_tiles=col_tiles, final_norm=final_norm),
        grid=(n // tm,),
        in_specs=[
            pl.BlockSpec((tm, d), lambda i: (i, 0)),
            _resident((1, d)),
            _resident((d, d_ff)),
            _resident((d, d_ff)),
            _resident((d_ff, d)),
            _resident((1, d)),
        ],
        out_specs=pl.BlockSpec((tm, d), lambda i: (i, 0)),
        out_shape=jax.ShapeDtypeStruct((n, d), F32),
        compiler_params=pltpu.CompilerParams(
            dimension_semantics=("arbitrary",), vmem_limit_bytes=vmem),
        name="ffn_final" if final_norm else "ffn",
    )(x2d, norm_g, w_gate, w_up, w_down, final_g)


def _iota2(shape, axis):
    return lax.broadcasted_iota(jnp.int32, shape, axis)


def _cumsum_rows(tri, x):
    hi, lo = _split_bf16(x)
    return _dot(tri, hi) + _dot(tri, lo)


def _segsum(x, ones2):
    hi, lo = _split_bf16(x)
    outs = []
    for j in range(x.shape[1] // LANES):
        sl = slice(j * LANES, (j + 1) * LANES)
        outs.append(_dot(hi[:, sl], ones2) + _dot(lo[:, sl], ones2))
    return jnp.concatenate(outs, axis=1)


def _block_rows(x, idx, nb):
    c = CHUNK
    return jnp.concatenate(
        [jnp.broadcast_to(x[b * c + idx:b * c + idx + 1], (c, x.shape[1])) for b in range(nb)], axis=0)


def _tile(x, b, j, width):
    return x[b * CHUNK:(b + 1) * CHUNK, j * width:(j + 1) * width]


def _assemble(tiles, nb, nj):
    return jnp.concatenate(
        [jnp.concatenate([tiles[(b, j)] for j in range(nj)], axis=1) for b in range(nb)], axis=0)


def _hgrn_chunk(p_ref, lb, tri, causal, hs_ref, nb):
    c = CHUNK
    w_a = lb.shape[1]
    hd = HA_HEAD_DIM
    q_a = p_ref[:, 0:w_a]
    f_a = p_ref[:, w_a:2 * w_a]
    v = p_ref[:, 2 * w_a:3 * w_a].astype(BF16)
    forget = lb + (1.0 - lb) * _sigmoid(f_a)
    log_f = jnp.log(forget)
    kh = 1.0 - forget
    q = q_a * _sigmoid(q_a)
    bc = _cumsum_rows(tri, log_f)
    b_ref = _block_rows(bc, c // 2, nb)
    b_last = _block_rows(bc, c - 1, nb)
    qt = (q * jnp.exp(bc - b_ref)).astype(BF16)
    kt = (kh * jnp.exp(b_ref - bc)).astype(BF16)
    kl = (kh * jnp.exp(b_last - bc)).astype(BF16)
    q0 = (q * jnp.exp(bc)).astype(BF16)
    d = jnp.exp(b_last)
    chains = [(b, h) for b in range(nb) for h in range(w_a // hd)]
    s_old = {ch: hs_ref[ch[0], ch[1]] for ch in chains}
    sc = {ch: jnp.where(causal, _dot_nt(_tile(qt, *ch, hd), _tile(kt, *ch, hd)), 0.0).astype(BF16)
          for ch in chains}
    o = {ch: _dot(sc[ch], _tile(v, *ch, hd)) + _dot_nt(_tile(q0, *ch, hd), s_old[ch].astype(BF16))
         for ch in chains}
    for ch in chains:
        hs_ref[ch[0], ch[1]] = (_tile(d, *ch, hd)[0:1] * s_old[ch]
                                + _dot_tn(_tile(v, *ch, hd), _tile(kl, *ch, hd)))
    o = {ch: t * lax.rsqrt(jnp.mean(t * t, axis=-1, keepdims=True) + NORM_EPS) for ch, t in o.items()}
    return _assemble(o, nb, w_a // hd)


def _rwkv_chunk(full, dl, rs_ref, nb, gmask, bdmask, lane_lo):
    c = CHUNK
    at, rt, bt, kt, a0, r0, bl, kl, v = full
    n_pairs = v.shape[1] // LANES
    pairs = [(b, pr) for b in range(nb) for pr in range(n_pairs)]
    chains = [(b, pr, hh) for (b, pr) in pairs for hh in range(2)]
    tl = lambda t, pair: _tile(t, pair[0], pair[1], LANES)
    head_mask = lambda hh: lane_lo if hh == 0 else jnp.logical_not(lane_lo)

    s_old = {pp: rs_ref[pp[0], pp[1]] for pp in pairs}
    z = {pp: _dot_nt(jnp.concatenate([tl(a0, pp), tl(r0, pp)], axis=0), s_old[pp].astype(BF16))
         for pp in pairs}
    zero = jnp.zeros((c, LANES), BF16)
    v_h = {ch: jnp.where(head_mask(ch[2]), tl(v, ch[:2]), zero) for ch in chains}
    g = {}
    for ch in chains:
        ar = jnp.concatenate([tl(at, ch[:2]), tl(rt, ch[:2])], axis=0)
        bk = jnp.concatenate([tl(bt, ch[:2]), tl(kt, ch[:2])], axis=0)
        g[ch] = jnp.where(gmask, _dot_nt(jnp.where(head_mask(ch[2]), ar, jnp.zeros_like(ar)), bk), 0.0)
    x = {}
    for ch in chains:
        lak = jnp.where(lane_lo, 0.0, g[ch][:c]).astype(BF16)
        x[ch] = (jnp.where(head_mask(ch[2]), z[ch[:2]][:c], 0.0)
                 + _dot(lak, jnp.concatenate([zero, v_h[ch]], axis=0)))
    pw = {ch: g[ch][:c, :c].astype(BF16) for ch in chains}
    n = 1
    while 2 * n < c:
        res = {ch: _dot(pw[ch], jnp.concatenate([x[ch].astype(BF16), pw[ch]], axis=1)) for ch in chains}
        x = {ch: x[ch] + res[ch][:, :LANES] for ch in chains}
        pw = {ch: res[ch][:, LANES:LANES + c].astype(BF16) for ch in chains}
        n *= 2
    x = {ch: x[ch] + _dot(pw[ch], x[ch].astype(BF16)) for ch in chains}
    yh = {ch: _dot(g[ch][c:].astype(BF16), jnp.concatenate([x[ch].astype(BF16), v_h[ch]], axis=0))
          for ch in chains}
    y = {}
    for pp in pairs:
        u_pair = x[pp + (0,)] + x[pp + (1,)]
        y[pp] = z[pp][c:] + yh[pp + (0,)] + yh[pp + (1,)]
        uv = jnp.concatenate([u_pair.astype(BF16), tl(v, pp)], axis=0)
        upd = _dot_tn(uv, jnp.concatenate([tl(bl, pp), tl(kl, pp)], axis=0))
        rs_ref[pp[0], pp[1]] = tl(dl, pp)[0:1] * s_old[pp] + jnp.where(bdmask, upd, 0.0)
    return _assemble(y, nb, n_pairs)


def _mixer_kernel(x_ref, mn_ref, win_ref, lbl_ref, hon_ref, mu_ref, w0_ref, a0_ref, wcomb_ref,
                  kk_ref, ka_ref, rk_ref, gnw_ref, gnb_ref, wout_ref, o_ref,
                  p_ref, hs_ref, rs_ref, carry_ref, *, nb, w_a, w_b):
    c = CHUNK
    rows = nb * c
    d_model = x_ref.shape[-1]
    n_rkv = 3 * w_b + LORA_PAD

    @pl.when(pl.program_id(1) == 0)
    def _():
        hs_ref[...] = jnp.zeros_like(hs_ref)
        rs_ref[...] = jnp.zeros_like(rs_ref)
        carry_ref[...] = jnp.zeros_like(carry_ref)

    x = x_ref[...].reshape(rows, d_model)
    h = _rms_norm(x, mn_ref[...]).astype(BF16)
    p_ref[...] = _dot(h, win_ref[...])

    logits = lbl_ref[...]
    e = jnp.exp(logits - jnp.max(logits, axis=0, keepdims=True))
    lb = e[0:1] / jnp.sum(e, axis=0, keepdims=True)

    ri, ci = _iota2((c, c), 0), _iota2((c, c), 1)
    causal = ci <= ri
    bi, bj = _iota2((rows, rows), 0), _iota2((rows, rows), 1)
    tri = ((bj <= bi) & (lax.shift_right_logical(bi, 6) == lax.shift_right_logical(bj, 6))).astype(BF16)
    gi, gj = _iota2((2 * c, 2 * c), 0), _iota2((2 * c, 2 * c), 1)
    ti, sj = gi & (c - 1), gj & (c - 1)
    gmask = (sj < ti) | ((gi >= c) & (sj == ti))
    li, lj = _iota2((LANES, LANES), 0), _iota2((LANES, LANES), 1)
    bdmask = lax.shift_right_logical(li, 6) == lax.shift_right_logical(lj, 6)
    ones2 = bdmask.astype(BF16)
    lane_lo = _iota2((1, LANES), 1) < HB_HEAD_DIM
    row0 = (_iota2((rows, 1), 0) & (c - 1)) == 0
    ll = _iota2((1, LORA_PAD), 1)

    o_a = _hgrn_chunk(p_ref, lb, tri, causal, hs_ref, nb)
    g_a = p_ref[:, 3 * w_a:4 * w_a]
    o_a = o_a * hon_ref[...] * (g_a * _sigmoid(g_a))

    pb = p_ref[:, 4 * w_a:4 * w_a + n_rkv]
    prev = jnp.concatenate(
        [jnp.broadcast_to(carry_ref[b:b + 1, :], (c, n_rkv)) for b in range(nb)], axis=0)
    shifted = jnp.where(row0, prev, pltpu.roll(pb, 1, 0))
    for b in range(nb):
        carry_ref[b:b + 1, :] = pb[(b + 1) * c - 1:(b + 1) * c, :]
    pb = pb + mu_ref[...] * (shifted - pb)
    r = pb[:, 0:w_b]
    k = pb[:, w_b:2 * w_b]
    v = pb[:, 2 * w_b:3 * w_b]
    low = pb[:, 3 * w_b:]
    act = jnp.where(ll < DECAY_LORA, jnp.tanh(low),
                    jnp.where(ll < DECAY_LORA + AAA_LORA, low,
                              jnp.where(ll < DECAY_LORA + AAA_LORA + GATE_LORA, _sigmoid(low), 0.0)))
    ld = _dot(act.astype(BF16), wcomb_ref[...])
    w_log = -_softplus(-(w0_ref[...] + ld[:, 0:w_b])) - 0.5
    lw = -jnp.exp(w_log)
    a = _sigmoid(a0_ref[...] + ld[:, w_b:2 * w_b])
    gate = ld[:, 2 * w_b:3 * w_b]
    kk = k * kk_ref[...]
    kk = kk / jnp.maximum(jnp.sqrt(_segsum(kk * kk, ones2)), L2_EPS)
    k = k * (1.0 + (a - 1.0) * ka_ref[...])
    bonus = _segsum(r * k * rk_ref[...], ones2) * v
    av, bv = -kk, kk * a

    gc = _cumsum_rows(tri, lw)
    gprev = gc - lw
    gref = _block_rows(gc, c // 2, nb)
    glast = _block_rows(gc, c - 1, nb)
    e_dn = jnp.exp(gref - gc)
    e_last = jnp.exp(glast - gc)
    full = (
        av * jnp.exp(gprev - gref), r * jnp.exp(gc - gref), bv * e_dn, k * e_dn,
        av * jnp.exp(gprev), r * jnp.exp(gc), bv * e_last, k * e_last, v)
    full = tuple(t.astype(BF16) for t in full)
    y = _rwkv_chunk(full, jnp.exp(glast), rs_ref, nb, gmask, bdmask, lane_lo)

    inv_n = 1.0 / HB_HEAD_DIM
    yc = y - _segsum(y, ones2) * inv_n
    var = _segsum(yc * yc, ones2) * inv_n
    yn = yc * lax.rsqrt(var + RWKV_GN_EPS) * gnw_ref[...] + gnb_ref[...]
    o_b = (yn + bonus) * gate

    o = jnp.concatenate([o_a, o_b], axis=1).astype(BF16)
    o_ref[...] = (x + _dot(o, wout_ref[...])).reshape(nb, c, d_model)


def _mixer(x, mix_norm, w_in, lb_logits, hgrn_out_norm, mu, w0, w2, a0, a2, g2, k_k, k_a, r_k,
           gn_w, gn_b, w_out):
    bsz, t, d = x.shape
    w_a = hgrn_out_norm.shape[-1]
    w_b = w0.shape[-1]
    nb, c = MIXER_BATCH_TILE, CHUNK
    n_lora = DECAY_LORA + AAA_LORA + GATE_LORA
    assert bsz % nb == 0 and t % c == 0
    assert w_a % HA_HEAD_DIM == 0 and w_b % LANES == 0
    assert w_in.shape[1] == 4 * w_a + 3 * w_b + n_lora
    pad = LORA_PAD - n_lora
    n_cols = 4 * w_a + 3 * w_b + LORA_PAD
    n_rkv = 3 * w_b + LORA_PAD
    w_in_p = jnp.pad(w_in, ((0, 0), (0, pad))).astype(BF16)
    mu_p = jnp.pad(mu.reshape(1, -1), ((0, 0), (0, pad)))
    wcomb = jnp.zeros((LORA_PAD, 3 * w_b), F32)
    wcomb = wcomb.at[0:DECAY_LORA, 0:w_b].set(w2)
    wcomb = wcomb.at[DECAY_LORA:DECAY_LORA + AAA_LORA, w_b:2 * w_b].set(a2)
    wcomb = wcomb.at[DECAY_LORA + AAA_LORA:n_lora, 2 * w_b:3 * w_b].set(g2)
    row = lambda vec: vec.reshape(1, -1).astype(F32)
    n_pairs = w_b // LANES
    n_heads_a = w_a // HA_HEAD_DIM
    state_bytes = nb * (n_heads_a * HA_HEAD_DIM * HA_HEAD_DIM + n_pairs * LANES * LANES) * 4
    weight_bytes = (d * n_cols + LORA_PAD * 3 * w_b + (w_a + w_b) * d) * 2
    vmem = weight_bytes + state_bytes + 3 * nb * c * n_cols * 4 + 4 * nb * c * d * 4 + 12 * MIB
    return pl.pallas_call(
        functools.partial(_mixer_kernel, nb=nb, w_a=w_a, w_b=w_b),
        grid=(bsz // nb, t // c),
        in_specs=[
            pl.BlockSpec((nb, c, d), lambda i, j: (i, j, 0)),
            _resident((1, d)),
            _resident((d, n_cols)),
            _resident(lb_logits.shape),
            _resident((1, w_a)),
            _resident((1, n_rkv)),
            _resident((1, w_b)),
            _resident((1, w_b)),
            _resident((LORA_PAD, 3 * w_b)),
            _resident((1, w_b)),
            _resident((1, w_b)),
            _resident((1, w_b)),
            _resident((1, w_b)),
            _resident((1, w_b)),
            _resident((w_a + w_b, d)),
        ],
        out_specs=pl.BlockSpec((nb, c, d), lambda i, j: (i, j, 0)),
        out_shape=jax.ShapeDtypeStruct((bsz, t, d), F32),
        scratch_shapes=[
            pltpu.VMEM((nb * c, n_cols), F32),
            pltpu.VMEM((nb, n_heads_a, HA_HEAD_DIM, HA_HEAD_DIM), F32),
            pltpu.VMEM((nb, n_pairs, LANES, LANES), F32),
            pltpu.VMEM((nb, n_rkv), F32),
        ],
        compiler_params=pltpu.CompilerParams(
            dimension_semantics=("arbitrary", "arbitrary"), vmem_limit_bytes=vmem),
        name="mixer",
    )(x, row(mix_norm), w_in_p, lb_logits.astype(F32), row(hgrn_out_norm), mu_p, row(w0), row(a0),
      wcomb.astype(BF16), row(k_k), row(k_a), row(r_k), row(gn_w), row(gn_b), w_out.astype(BF16))


def kernel(x, ffn1_norm, ffn1_w_gate, ffn1_w_up, ffn1_w_down, mix_norm, w_in, hgrn_lb_logits, hgrn_out_norm, rwkv_shift_mu, rwkv_w0, rwkv_w2, rwkv_a0, rwkv_a2, rwkv_g2, rwkv_k_k, rwkv_k_a, rwkv_r_k, rwkv_gn_w, rwkv_gn_b, w_out, ffn2_norm, ffn2_w_gate, ffn2_w_up, ffn2_w_down, final_norm):
    bsz, t, d = x.shape
    depth = ffn1_norm.shape[0]
    assert depth == 1 and hgrn_lb_logits.shape[0] == depth + 1
    l = 0
    row = lambda vec: vec.reshape(1, -1).astype(F32)
    fg = row(final_norm)
    h = _ffn(x.reshape(bsz * t, d), row(ffn1_norm[l]), ffn1_w_gate[l].astype(BF16),
             ffn1_w_up[l].astype(BF16), ffn1_w_down[l].astype(BF16), fg, final_norm=False)
    h = _mixer(h.reshape(bsz, t, d), mix_norm[l], w_in[l], hgrn_lb_logits, hgrn_out_norm[l],
               rwkv_shift_mu[l], rwkv_w0[l], rwkv_w2[l], rwkv_a0[l], rwkv_a2[l], rwkv_g2[l],
               rwkv_k_k[l], rwkv_k_a[l], rwkv_r_k[l], rwkv_gn_w[l], rwkv_gn_b[l], w_out[l])
    h = _ffn(h.reshape(bsz * t, d), row(ffn2_norm[l]), ffn2_w_gate[l].astype(BF16),
             ffn2_w_up[l].astype(BF16), ffn2_w_down[l].astype(BF16), fg, final_norm=True)
    return h.reshape(bsz, t, d)
```

```python
import functools

import jax
import jax.numpy as jnp
from jax import lax
from jax.experimental import pallas as pl
from jax.experimental.pallas import tpu as pltpu

F32 = jnp.float32
BF16 = jnp.bfloat16

NORM_EPS = 1e-6
RWKV_GN_EPS = 64e-5
L2_EPS = 1e-12

LANES = 128
MXU_COLS = 256
HA_HEAD_DIM = 128
HB_HEAD_DIM = 64
CHUNK = 64
DECAY_LORA, AAA_LORA, GATE_LORA = 32, 32, 96
LORA_PAD = 256
FFN_ROW_TILE = 512
FFN_COL_TILE = 768
MIXER_GROUP = 2
MIXER_GROUPS_PER_STEP = 2
MIB = 1024 * 1024


def _dot(a, b):
    return jnp.dot(a, b, preferred_element_type=F32)


def _dot_nt(a, b):
    return lax.dot_general(a, b, (((1,), (1,)), ((), ())), preferred_element_type=F32)


def _dot_tn(a, b):
    return lax.dot_general(a, b, (((0,), (0,)), ((), ())), preferred_element_type=F32)


def _split_bf16(x):
    hi = x.astype(BF16)
    lo = (x - hi.astype(F32)).astype(BF16)
    return hi, lo


def _rms_norm(x, g):
    return x * lax.rsqrt(jnp.mean(x * x, axis=-1, keepdims=True) + NORM_EPS) * g


def _sigmoid(x):
    return 1.0 / (1.0 + jnp.exp(-x))


def _softplus(x):
    return jnp.maximum(x, 0.0) + jnp.log1p(jnp.exp(-jnp.abs(x)))


def _ffn_kernel(x_ref, g_ref, wg_ref, wu_ref, wd_ref, fn_ref, o_ref, *, col_tiles, final_norm):
    x = x_ref[...]
    h = _rms_norm(x, g_ref[...]).astype(BF16)
    acc = jnp.zeros(x.shape, F32)
    for lo, hi in col_tiles:
        gate = _dot(h, wg_ref[:, lo:hi])
        up = _dot(h, wu_ref[:, lo:hi])
        act = (gate * _sigmoid(gate) * up).astype(BF16)
        acc = acc + _dot(act, wd_ref[lo:hi, :])
    y = x + 0.5 * acc
    if final_norm:
        y = _rms_norm(y, fn_ref[...])
    o_ref[...] = y


def _resident(shape):
    return pl.BlockSpec(shape, lambda *_: (0,) * len(shape), pipeline_mode=pl.Buffered(1))


def _ffn(x2d, norm_g, w_gate, w_up, w_down, final_g, *, final_norm):
    n, d = x2d.shape
    d_ff = w_gate.shape[1]
    tm = FFN_ROW_TILE
    assert n % tm == 0 and d_ff % MXU_COLS == 0
    col_tiles = tuple((lo, min(lo + FFN_COL_TILE, d_ff)) for lo in range(0, d_ff, FFN_COL_TILE))
    weight_bytes = 3 * d * d_ff * 2
    tile_bytes = tm * d * 4
    vmem = weight_bytes + 8 * tile_bytes + 3 * tm * FFN_COL_TILE * 4 + 8 * MIB
    return pl.pallas_call(
        functools.partial(_ffn_kernel, col_tiles=col_tiles, final_norm=final_norm),
        grid=(n // tm,),
        in_specs=[
            pl.BlockSpec((tm, d), lambda i: (i, 0)),
            _resident((1, d)),
            _resident((d, d_ff)),
            _resident((d, d_ff)),
            _resident((d_ff, d)),
            _resident((1, d)),
        ],
        out_specs=pl.BlockSpec((tm, d), lambda i: (i, 0)),
        out_shape=jax.ShapeDtypeStruct((n, d), F32),
        compiler_params=pltpu.CompilerParams(
            dimension_semantics=("arbitrary",), vmem_limit_bytes=vmem),
        name="ffn_final" if final_norm else "ffn",
    )(x2d, norm_g, w_gate, w_up, w_down, final_g)


def _iota2(shape, axis):
    return lax.broadcasted_iota(jnp.int32, shape, axis)


def _cumsum_rows(tri, x):
    hi, lo = _split_bf16(x)
    return _dot(tri, hi) + _dot(tri, lo)


def _segsum(x, lane_lo):
    outs = []
    for j in range(x.shape[1] // LANES):
        t = x[:, j * LANES:(j + 1) * LANES]
        lo = jnp.sum(jnp.where(lane_lo, t, 0.0), axis=-1, keepdims=True)
        hi = jnp.sum(jnp.where(lane_lo, 0.0, t), axis=-1, keepdims=True)
        outs.append(jnp.where(lane_lo, lo, hi))
    return jnp.concatenate(outs, axis=1)


def _block_rows(x, idx):
    c = CHUNK
    return jnp.concatenate(
        [jnp.broadcast_to(x[r + idx:r + idx + 1], (c, x.shape[1])) for r in range(0, x.shape[0], c)],
        axis=0)


def _swap_halves(t):
    return jnp.concatenate(
        [pltpu.roll(t[:, j * LANES:(j + 1) * LANES], HB_HEAD_DIM, 1) for j in range(t.shape[1] // LANES)],
        axis=1)


def _tile(x, b, j, width):
    return x[b * CHUNK:(b + 1) * CHUNK, j * width:(j + 1) * width]


def _assemble(tiles, nb, nj):
    return jnp.concatenate(
        [jnp.concatenate([tiles[(b, j)] for j in range(nj)], axis=1) for b in range(nb)], axis=0)


def _merge_stages(*stage_lists):
    tagged = []
    for stages in stage_lists:
        tagged += [((i + 0.5) / len(stages), i, fn) for i, fn in enumerate(stages)]
    for _, _, fn in sorted(tagged, key=lambda t: t[:2]):
        fn()


class _Masks:
    def __init__(self, rows):
        c = CHUNK
        ri, ci = _iota2((c, c), 0), _iota2((c, c), 1)
        self.causal = ci <= ri
        bi, bj = _iota2((rows, rows), 0), _iota2((rows, rows), 1)
        same_block = lax.shift_right_logical(bi, 6) == lax.shift_right_logical(bj, 6)
        self.tri = ((bj <= bi) & same_block).astype(BF16)
        gi, gj = _iota2((2 * c, 4 * c), 0), _iota2((2 * c, 4 * c), 1)
        ti, sj = gi & (c - 1), gj & (c - 1)
        self.gmask = (sj < ti) | ((gi >= c) & (sj == ti))
        li, lj = _iota2((LANES, LANES), 0), _iota2((LANES, LANES), 1)
        self.admask = lax.shift_right_logical(li, 6) != lax.shift_right_logical(lj, 6)
        self.lane_lo = _iota2((1, LANES), 1) < HB_HEAD_DIM
        self.row0 = (_iota2((rows, 1), 0) & (c - 1)) == 0
        self.lora_lane = _iota2((1, LORA_PAD), 1)


def _hgrn_stages(ctx, hs_ref, b0, nb, w_a, mk):
    hd = HA_HEAD_DIM
    chains = [(b, h) for b in range(nb) for h in range(w_a // hd)]
    st = {}

    def scores():
        qt, kt = ctx['hg_qt'], ctx['hg_kt']
        st['s_old'] = {ch: hs_ref[b0 + ch[0], ch[1]] for ch in chains}
        st['sc'] = {ch: jnp.where(mk.causal, _dot_nt(_tile(qt, *ch, hd), _tile(kt, *ch, hd)), 0.0).astype(BF16)
                    for ch in chains}

    def outputs():
        v, q0, kl, d = ctx['hg_v'], ctx['hg_q0'], ctx['hg_kl'], ctx['hg_d']
        s_old = st['s_old']
        o = {ch: _dot(st['sc'][ch], _tile(v, *ch, hd)) + _dot_nt(_tile(q0, *ch, hd), s_old[ch].astype(BF16))
             for ch in chains}
        for ch in chains:
            hs_ref[b0 + ch[0], ch[1]] = (_tile(d, *ch, hd)[0:1] * s_old[ch]
                                         + _dot_tn(_tile(v, *ch, hd), _tile(kl, *ch, hd)))
        o = {ch: t * lax.rsqrt(jnp.mean(t * t, axis=-1, keepdims=True) + NORM_EPS) for ch, t in o.items()}
        ctx['o_a'] = _assemble(o, nb, w_a // hd)

    return [scores, outputs]


def _rwkv_stages(ctx, rs_ref, b0, nb, w_b, mk):
    c = CHUNK
    n_pairs = w_b // LANES
    lane_lo = mk.lane_lo
    pairs = [(b, pr) for b in range(nb) for pr in range(n_pairs)]
    tl = lambda t, pair: _tile(t, pair[0], pair[1], LANES)
    zero = jnp.zeros((c, LANES), BF16)
    lo = lambda t: jnp.where(lane_lo, t, zero)
    hi = lambda t: jnp.where(lane_lo, zero, t)
    cat0 = lambda *ts: jnp.concatenate(ts, axis=0)
    st = {}

    def state_read():
        at, rt, bt, kt, a0, r0, bl, kl, v = ctx['rw']
        st['s_old'] = {pp: rs_ref[b0 + pp[0], pp[1]] for pp in pairs}
        st['z'] = {pp: _dot_nt(cat0(tl(a0, pp), tl(r0, pp)), st['s_old'][pp].astype(BF16)) for pp in pairs}

    def gram():
        at, rt, bt, kt, a0, r0, bl, kl, v = ctx['rw']
        g = {}
        for pp in pairs:
            b_, k_ = tl(bt, pp), tl(kt, pp)
            g[pp] = jnp.where(
                mk.gmask, _dot_nt(cat0(tl(at, pp), tl(rt, pp)), cat0(lo(b_), lo(k_), hi(k_), hi(b_))), 0.0)
        st['g'] = g

    def rhs0():
        v = ctx['rw'][8]
        x, pw = {}, {}
        for pp in pairs:
            t0, t1 = st['g'][pp][:c, :LANES], st['g'][pp][:c, LANES:]
            pw[pp] = jnp.where(lane_lo, t0, t1).astype(BF16)
            lak = jnp.where(lane_lo, t1, t0).astype(BF16)
            x[pp] = st['z'][pp][:c] + _dot(lak, cat0(lo(tl(v, pp)), hi(tl(v, pp))))
        st['x'], st['pw'] = x, pw

    def solve_step():
        x, pw, res = st['x'], st['pw'], {}
        for pp in pairs:
            xb = x[pp].astype(BF16)
            q0 = jnp.where(lane_lo, pw[pp], xb)
            q1 = jnp.where(lane_lo, xb, pw[pp])
            rhs = cat0(jnp.concatenate([q0, zero], axis=1), jnp.concatenate([zero, q1], axis=1))
            res[pp] = _dot(pw[pp], rhs)
        st['x'] = {pp: x[pp] + jnp.where(lane_lo, res[pp][:, LANES:], res[pp][:, :LANES]) for pp in pairs}
        st['pw'] = {pp: jnp.where(lane_lo, res[pp][:, :LANES], res[pp][:, LANES:]).astype(BF16)
                    for pp in pairs}

    def solve_last():
        x, pw = st['x'], st['pw']
        for pp in pairs:
            xb = x[pp].astype(BF16)
            x[pp] = x[pp] + _dot(pw[pp], cat0(hi(xb), lo(xb)))

    def outputs():
        at, rt, bt, kt, a0, r0, bl, kl, v = ctx['rw']
        dl = ctx['rw_dl']
        y = {}
        for pp in pairs:
            ub, vv = st['x'][pp].astype(BF16), tl(v, pp)
            y[pp] = st['z'][pp][c:] + _dot(st['g'][pp][c:].astype(BF16), cat0(hi(ub), hi(vv), lo(vv), lo(ub)))
            upd = _dot_tn(cat0(ub, vv), cat0(tl(bl, pp), tl(kl, pp)))
            rs_ref[b0 + pp[0], pp[1]] = tl(dl, pp)[0:1] * st['s_old'][pp] + jnp.where(mk.admask, upd, 0.0)
        ctx['y'] = _assemble(y, nb, n_pairs)

    n_steps = 0
    n = 1
    while 2 * n < c:
        n_steps += 1
        n *= 2
    return [state_read, gram, rhs0] + [solve_step] * n_steps + [solve_last, outputs]


def _mixer_kernel(x_ref, mn_ref, win_ref, lbl_ref, hon_ref, mu_ref, w0_ref, a0_ref, wcomb_ref,
                  kk_ref, ka_ref, rk_ref, gnw_ref, gnb_ref, wout_ref, o_ref,
                  p_ref, hs_ref, rs_ref, carry_ref, *, nb, n_groups, w_a, w_b):
    c = CHUNK
    rows = nb * c
    d_model = x_ref.shape[-1]
    n_rkv = 3 * w_b + LORA_PAD

    @pl.when(pl.program_id(1) == 0)
    def _():
        hs_ref[...] = jnp.zeros_like(hs_ref)
        rs_ref[...] = jnp.zeros_like(rs_ref)
        carry_ref[...] = jnp.zeros_like(carry_ref)

    mk = _Masks(rows)
    logits = lbl_ref[...]
    e = jnp.exp(logits - jnp.max(logits, axis=0, keepdims=True))
    lb = e[0:1] / jnp.sum(e, axis=0, keepdims=True)

    def prep_stages(grp, ctx):
        b0 = grp * nb
        prow = slice(grp * rows, (grp + 1) * rows)

        def in_proj():
            x = x_ref[b0:b0 + nb].reshape(rows, d_model)
            h = _rms_norm(x, mn_ref[...]).astype(BF16)
            p_ref[prow, :] = _dot(h, win_ref[...])

        def hgrn_gates():
            q_a = p_ref[prow, 0:w_a]
            f_a = p_ref[prow, w_a:2 * w_a]
            forget = lb + (1.0 - lb) * _sigmoid(f_a)
            log_f = jnp.log(forget)
            ctx['hg_kh'] = 1.0 - forget
            ctx['hg_q'] = q_a * _sigmoid(q_a)
            ctx['hg_bc'] = _cumsum_rows(mk.tri, log_f)

        def hgrn_decays():
            bc, q, kh = ctx['hg_bc'], ctx['hg_q'], ctx['hg_kh']
            b_ref = _block_rows(bc, c // 2)
            b_last = _block_rows(bc, c - 1)
            ctx['hg_qt'] = (q * jnp.exp(bc - b_ref)).astype(BF16)
            ctx['hg_kt'] = (kh * jnp.exp(b_ref - bc)).astype(BF16)
            ctx['hg_kl'] = (kh * jnp.exp(b_last - bc)).astype(BF16)
            ctx['hg_q0'] = (q * jnp.exp(bc)).astype(BF16)
            ctx['hg_d'] = jnp.exp(b_last)
            ctx['hg_v'] = p_ref[prow, 2 * w_a:3 * w_a].astype(BF16)

        def rwkv_shift():
            pb = p_ref[prow, 4 * w_a:4 * w_a + n_rkv]
            prev = jnp.concatenate(
                [jnp.broadcast_to(carry_ref[b0 + b:b0 + b + 1, :], (c, n_rkv)) for b in range(nb)], axis=0)
            shifted = jnp.where(mk.row0, prev, pltpu.roll(pb, 1, 0))
            for b in range(nb):
                carry_ref[b0 + b:b0 + b + 1, :] = pb[(b + 1) * c - 1:(b + 1) * c, :]
            pb = pb + mu_ref[...] * (shifted - pb)
            ctx['r'] = pb[:, 0:w_b]
            ctx['k'] = pb[:, w_b:2 * w_b]
            ctx['v'] = pb[:, 2 * w_b:3 * w_b]
            low = pb[:, 3 * w_b:]
            ll = mk.lora_lane
            act = jnp.where(ll < DECAY_LORA, jnp.tanh(low),
                            jnp.where(ll < DECAY_LORA + AAA_LORA, low,
                                      jnp.where(ll < DECAY_LORA + AAA_LORA + GATE_LORA, _sigmoid(low), 0.0)))
            ctx['ld'] = _dot(act.astype(BF16), wcomb_ref[...])

        def rwkv_keys():
            r, k, v, ld = ctx['r'], ctx['k'], ctx['v'], ctx['ld']
            w_log = -_softplus(-(w0_ref[...] + ld[:, 0:w_b])) - 0.5
            lw = -jnp.exp(w_log)
            a = _sigmoid(a0_ref[...] + ld[:, w_b:2 * w_b])
            ctx['gate'] = ld[:, 2 * w_b:3 * w_b]
            kk = k * kk_ref[...]
            kk = kk / jnp.maximum(jnp.sqrt(_segsum(kk * kk, mk.lane_lo)), L2_EPS)
            k = k * (1.0 + (a - 1.0) * ka_ref[...])
            ctx['bonus'] = _segsum(r * k * rk_ref[...], mk.lane_lo) * v
            ctx['k'], ctx['av'], ctx['bv'], ctx['lw'] = k, -kk, kk * a, lw
            ctx['gc'] = _cumsum_rows(mk.tri, lw)

        def rwkv_decays():
            r, k, v, av, bv, lw, gc = (ctx[n] for n in ('r', 'k', 'v', 'av', 'bv', 'lw', 'gc'))
            gprev = gc - lw
            gref = _block_rows(gc, c // 2)
            glast = _block_rows(gc, c - 1)
            e_dn = jnp.exp(gref - gc)
            e_last = jnp.exp(glast - gc)
            full = (
                av * jnp.exp(gprev - gref), r * jnp.exp(gc - gref), bv * e_dn, k * e_dn,
                av * jnp.exp(gprev), r * jnp.exp(gc), bv * e_last, k * e_last, _swap_halves(v))
            ctx['rw'] = tuple(t.astype(BF16) for t in full)
            ctx['rw_dl'] = jnp.exp(glast)

        return [in_proj, hgrn_gates, hgrn_decays, rwkv_shift, rwkv_keys, rwkv_decays]

    def chain_stages(grp, ctx):
        b0 = grp * nb
        return _hgrn_stages(ctx, hs_ref, b0, nb, w_a, mk) + _rwkv_stages(ctx, rs_ref, b0, nb, w_b, mk)

    def post_stages(grp, ctx):
        b0 = grp * nb
        prow = slice(grp * rows, (grp + 1) * rows)

        def group_norm():
            g_a = p_ref[prow, 3 * w_a:4 * w_a]
            ctx['o_a'] = ctx['o_a'] * hon_ref[...] * (g_a * _sigmoid(g_a))
            y = _swap_halves(ctx['y'])
            inv_n = 1.0 / HB_HEAD_DIM
            yc = y - _segsum(y, mk.lane_lo) * inv_n
            var = _segsum(yc * yc, mk.lane_lo) * inv_n
            yn = yc * lax.rsqrt(var + RWKV_GN_EPS) * gnw_ref[...] + gnb_ref[...]
            ctx['o_b'] = (yn + ctx['bonus']) * ctx['gate']

        def out_proj():
            o = jnp.concatenate([ctx['o_a'], ctx['o_b']], axis=1).astype(BF16)
            x = x_ref[b0:b0 + nb].reshape(rows, d_model)
            o_ref[b0:b0 + nb] = (x + _dot(o, wout_ref[...])).reshape(nb, c, d_model)

        return [group_norm, out_proj]

    ctxs = [dict() for _ in range(n_groups)]
    _merge_stages(prep_stages(0, ctxs[0]))
    for grp in range(n_groups):
        lists = [chain_stages(grp, ctxs[grp])]
        if grp + 1 < n_groups:
            lists.append(prep_stages(grp + 1, ctxs[grp + 1]))
        if grp >= 1:
            lists.append(post_stages(grp - 1, ctxs[grp - 1]))
        _merge_stages(*lists)
    _merge_stages(post_stages(n_groups - 1, ctxs[n_groups - 1]))


def _mixer(x, mix_norm, w_in, lb_logits, hgrn_out_norm, mu, w0, w2, a0, a2, g2, k_k, k_a, r_k,
           gn_w, gn_b, w_out):
    bsz, t, d = x.shape
    w_a = hgrn_out_norm.shape[-1]
    w_b = w0.shape[-1]
    nb, n_groups, c = MIXER_GROUP, MIXER_GROUPS_PER_STEP, CHUNK
    nbs = nb * n_groups
    n_lora = DECAY_LORA + AAA_LORA + GATE_LORA
    assert bsz % nbs == 0 and t % c == 0
    assert w_a % HA_HEAD_DIM == 0 and w_b % LANES == 0
    assert w_in.shape[1] == 4 * w_a + 3 * w_b + n_lora
    pad = LORA_PAD - n_lora
    n_cols = 4 * w_a + 3 * w_b + LORA_PAD
    n_rkv = 3 * w_b + LORA_PAD
    w_in_p = jnp.pad(w_in, ((0, 0), (0, pad))).astype(BF16)
    mu_p = jnp.pad(mu.reshape(1, -1), ((0, 0), (0, pad)))
    wcomb = jnp.zeros((LORA_PAD, 3 * w_b), F32)
    wcomb = wcomb.at[0:DECAY_LORA, 0:w_b].set(w2)
    wcomb = wcomb.at[DECAY_LORA:DECAY_LORA + AAA_LORA, w_b:2 * w_b].set(a2)
    wcomb = wcomb.at[DECAY_LORA + AAA_LORA:n_lora, 2 * w_b:3 * w_b].set(g2)
    row = lambda vec: vec.reshape(1, -1).astype(F32)
    n_pairs = w_b // LANES
    n_heads_a = w_a // HA_HEAD_DIM
    state_bytes = nbs * (n_heads_a * HA_HEAD_DIM * HA_HEAD_DIM + n_pairs * LANES * LANES) * 4
    weight_bytes = (d * n_cols + LORA_PAD * 3 * w_b + (w_a + w_b) * d) * 2
    vmem = weight_bytes + state_bytes + 3 * nbs * c * n_cols * 4 + 4 * nbs * c * d * 4 + 12 * MIB
    return pl.pallas_call(
        functools.partial(_mixer_kernel, nb=nb, n_groups=n_groups, w_a=w_a, w_b=w_b),
        grid=(bsz // nbs, t // c),
        in_specs=[
            pl.BlockSpec((nbs, c, d), lambda i, j: (i, j, 0)),
            _resident((1, d)),
            _resident((d, n_cols)),
            _resident(lb_logits.shape),
            _resident((1, w_a)),
            _resident((1, n_rkv)),
            _resident((1, w_b)),
            _resident((1, w_b)),
            _resident((LORA_PAD, 3 * w_b)),
            _resident((1, w_b)),
            _resident((1, w_b)),
            _resident((1, w_b)),
            _resident((1, w_b)),
            _resident((1, w_b)),
            _resident((w_a + w_b, d)),
        ],
        out_specs=pl.BlockSpec((nbs, c, d), lambda i, j: (i, j, 0)),
        out_shape=jax.ShapeDtypeStruct((bsz, t, d), F32),
        scratch_shapes=[
            pltpu.VMEM((nbs * c, n_cols), F32),
            pltpu.VMEM((nbs, n_heads_a, HA_HEAD_DIM, HA_HEAD_DIM), F32),
            pltpu.VMEM((nbs, n_pairs, LANES, LANES), F32),
            pltpu.VMEM((nbs, n_rkv), F32),
        ],
        compiler_params=pltpu.CompilerParams(
            dimension_semantics=("arbitrary", "arbitrary"), vmem_limit_bytes=vmem),
        name="mixer",
    )(x, row(mix_norm), w_in_p, lb_logits.astype(F32), row(hgrn_out_norm), mu_p, row(w0), row(a0),
      wcomb.astype(BF16), row(k_k), row(k_a), row(r_k), row(gn_w), row(gn_b), w_out.astype(BF16))


def kernel(x, ffn1_norm, ffn1_w_gate, ffn1_w_up, ffn1_w_down, mix_norm, w_in, hgrn_lb_logits, hgrn_out_norm, rwkv_shift_mu, rwkv_w0, rwkv_w2, rwkv_a0, rwkv_a2, rwkv_g2, rwkv_k_k, rwkv_k_a, rwkv_r_k, rwkv_gn_w, rwkv_gn_b, w_out, ffn2_norm, ffn2_w_gate, ffn2_w_up, ffn2_w_down, final_norm):
    bsz, t, d = x.shape
    depth = ffn1_norm.shape[0]
    assert depth == 1 and hgrn_lb_logits.shape[0] == depth + 1
    l = 0
    row = lambda vec: vec.reshape(1, -1).astype(F32)
    fg = row(final_norm)
    h = _ffn(x.reshape(bsz * t, d), row(ffn1_norm[l]), ffn1_w_gate[l].astype(BF16),
             ffn1_w_up[l].astype(BF16), ffn1_w_down[l].astype(BF16), fg, final_norm=False)
    h = _mixer(h.reshape(bsz, t, d), mix_norm[l], w_in[l], hgrn_lb_logits, hgrn_out_norm[l],
               rwkv_shift_mu[l], rwkv_w0[l], rwkv_w2[l], rwkv_a0[l], rwkv_a2[l], rwkv_g2[l],
               rwkv_k_k[l], rwkv_k_a[l], rwkv_r_k[l], rwkv_gn_w[l], rwkv_gn_b[l], w_out[l])
    h = _ffn(h.reshape(bsz * t, d), row(ffn2_norm[l]), ffn2_w_gate[l].astype(BF16),
             ffn2_w_up[l].astype(BF16), ffn2_w_down[l].astype(BF16), fg, final_norm=True)
    return h.reshape(bsz, t, d)
```

```python
import functools

import jax
import jax.numpy as jnp
from jax import lax
from jax.experimental import pallas as pl
from jax.experimental.pallas import tpu as pltpu

F32 = jnp.float32
BF16 = jnp.bfloat16

NORM_EPS = 1e-6
RWKV_GN_EPS = 64e-5
L2_EPS = 1e-12

LANES = 128
MXU_COLS = 256
HA_HEAD_DIM = 128
HB_HEAD_DIM = 64
CHUNK = 64
DECAY_LORA, AAA_LORA, GATE_LORA = 32, 32, 96
LORA_PAD = 256
FFN_ROW_TILE = 512
FFN_COL_TILE = 768
MIXER_GROUP = 4
MIXER_GROUPS_PER_STEP = 2
IN_PROJ_COL_TILE = 512
MIB = 1024 * 1024


def _dot(a, b):
    return jnp.dot(a, b, preferred_element_type=F32)


def _dot_nt(a, b):
    return lax.dot_general(a, b, (((1,), (1,)), ((), ())), preferred_element_type=F32)


def _dot_tn(a, b):
    return lax.dot_general(a, b, (((0,), (0,)), ((), ())), preferred_element_type=F32)


def _split_bf16(x):
    hi = x.astype(BF16)
    lo = (x - hi.astype(F32)).astype(BF16)
    return hi, lo


def _rms_norm(x, g):
    return x * lax.rsqrt(jnp.mean(x * x, axis=-1, keepdims=True) + NORM_EPS) * g


def _sigmoid(x):
    return 0.5 * jnp.tanh(0.5 * x) + 0.5


def _softplus(x):
    return jnp.maximum(x, 0.0) + jnp.log1p(jnp.exp(-jnp.abs(x)))


def _ffn_kernel(x_ref, g_ref, wg_ref, wu_ref, wd_ref, fn_ref, o_ref, *, col_tiles, final_norm):
    x = x_ref[...]
    h = _rms_norm(x, g_ref[...]).astype(BF16)
    acc = jnp.zeros(x.shape, F32)
    for lo, hi in col_tiles:
        gate = _dot(h, wg_ref[:, lo:hi])
        up = _dot(h, wu_ref[:, lo:hi])
        act = (gate * _sigmoid(gate) * up).astype(BF16)
        acc = acc + _dot(act, wd_ref[lo:hi, :])
    y = x + 0.5 * acc
    if final_norm:
        y = _rms_norm(y, fn_ref[...])
    o_ref[...] = y


def _resident(shape):
    return pl.BlockSpec(shape, lambda *_: (0,) * len(shape), pipeline_mode=pl.Buffered(1))


def _ffn(x2d, norm_g, w_gate, w_up, w_down, final_g, *, final_norm):
    n, d = x2d.shape
    d_ff = w_gate.shape[1]
    tm = FFN_ROW_TILE
    assert n % tm == 0 and d_ff % MXU_COLS == 0
    col_tiles = tuple((lo, min(lo + FFN_COL_TILE, d_ff)) for lo in range(0, d_ff, FFN_COL_TILE))
    weight_bytes = 3 * d * d_ff * 2
    tile_bytes = tm * d * 4
    vmem = weight_bytes + 8 * tile_bytes + 3 * tm * FFN_COL_TILE * 4 + 8 * MIB
    return pl.pallas_call(
        functools.partial(_ffn_kernel, col_tiles=col_tiles, final_norm=final_norm),
        grid=(n // tm,),
        in_specs=[
            pl.BlockSpec((tm, d), lambda i: (i, 0)),
            _resident((1, d)),
            _resident((d, d_ff)),
            _resident((d, d_ff)),
            _resident((d_ff, d)),
            _resident((1, d)),
        ],
        out_specs=pl.BlockSpec((tm, d), lambda i: (i, 0)),
        out_shape=jax.ShapeDtypeStruct((n, d), F32),
        compiler_params=pltpu.CompilerParams(
            dimension_semantics=("arbitrary",), vmem_limit_bytes=vmem),
        name="ffn_final" if final_norm else "ffn",
    )(x2d, norm_g, w_gate, w_up, w_down, final_g)


def _iota2(shape, axis):
    return lax.broadcasted_iota(jnp.int32, shape, axis)


def _cumsum_rows(tri, x):
    hi, lo = _split_bf16(x)
    return _dot(tri, hi) + _dot(tri, lo)


def _segsum(x, lane_lo):
    outs = []
    for j in range(x.shape[1] // LANES):
        t = x[:, j * LANES:(j + 1) * LANES]
        lo = jnp.sum(jnp.where(lane_lo, t, 0.0), axis=-1, keepdims=True)
        hi = jnp.sum(jnp.where(lane_lo, 0.0, t), axis=-1, keepdims=True)
        outs.append(jnp.where(lane_lo, lo, hi))
    return jnp.concatenate(outs, axis=1)


def _block_rows(x, idx, fn=None):
    c = CHUNK
    outs = []
    for r in range(0, x.shape[0], c):
        row = x[r + idx:r + idx + 1]
        if fn is not None:
            row = fn(row, x[r + c // 2:r + c // 2 + 1])
        outs.append(jnp.broadcast_to(row, (c, x.shape[1])))
    return jnp.concatenate(outs, axis=0)


def _swap_halves(t):
    return jnp.concatenate(
        [pltpu.roll(t[:, j * LANES:(j + 1) * LANES], HB_HEAD_DIM, 1) for j in range(t.shape[1] // LANES)],
        axis=1)


def _tile(x, b, j, width):
    return x[b * CHUNK:(b + 1) * CHUNK, j * width:(j + 1) * width]


def _assemble(tiles, nb, nj):
    return jnp.concatenate(
        [jnp.concatenate([tiles[(b, j)] for j in range(nj)], axis=1) for b in range(nb)], axis=0)


def _merge_stages(*stage_lists):
    tagged = []
    for stages in stage_lists:
        tagged += [((i + 0.5) / len(stages), i, fn) for i, fn in enumerate(stages)]
    for _, _, fn in sorted(tagged, key=lambda t: t[:2]):
        fn()


class _Masks:
    def __init__(self, rows):
        c = CHUNK
        ri, ci = _iota2((c, c), 0), _iota2((c, c), 1)
        self.causal = ci <= ri
        bi, bj = _iota2((rows, rows), 0), _iota2((rows, rows), 1)
        same_block = lax.shift_right_logical(bi, 6) == lax.shift_right_logical(bj, 6)
        self.tri = ((bj <= bi) & same_block).astype(BF16)
        gi, gj = _iota2((2 * c, 4 * c), 0), _iota2((2 * c, 4 * c), 1)
        ti, sj = gi & (c - 1), gj & (c - 1)
        self.gmask = (sj < ti) | ((gi >= c) & (sj == ti))
        li, lj = _iota2((LANES, LANES), 0), _iota2((LANES, LANES), 1)
        self.admask = lax.shift_right_logical(li, 6) != lax.shift_right_logical(lj, 6)
        self.lane_lo = _iota2((1, LANES), 1) < HB_HEAD_DIM
        self.row0 = (_iota2((rows, 1), 0) & (c - 1)) == 0
        self.lora_lane = _iota2((1, LORA_PAD), 1)


def _hgrn_stages(ctx, hs_ref, b0, nb, w_a, mk):
    hd = HA_HEAD_DIM
    chains = [(b, h) for b in range(nb) for h in range(w_a // hd)]
    st = {}

    def scores():
        qt, kt = ctx['hg_qt'], ctx['hg_kt']
        st['s_old'] = {ch: hs_ref[b0 + ch[0], ch[1]] for ch in chains}
        st['sc'] = {ch: jnp.where(mk.causal, _dot_nt(_tile(qt, *ch, hd), _tile(kt, *ch, hd)), 0.0).astype(BF16)
                    for ch in chains}

    def outputs():
        v, q0, kl, d = ctx['hg_v'], ctx['hg_q0'], ctx['hg_kl'], ctx['hg_d']
        s_old = st['s_old']
        o = {ch: _dot(st['sc'][ch], _tile(v, *ch, hd)) + _dot_nt(_tile(q0, *ch, hd), s_old[ch].astype(BF16))
             for ch in chains}
        for ch in chains:
            hs_ref[b0 + ch[0], ch[1]] = (_tile(d, *ch, hd)[0:1] * s_old[ch]
                                         + _dot_tn(_tile(v, *ch, hd), _tile(kl, *ch, hd)))
        st['o'] = o

    def normalise():
        o = {ch: t * lax.rsqrt(jnp.mean(t * t, axis=-1, keepdims=True) + NORM_EPS) for ch, t in st['o'].items()}
        ctx['o_a'] = _assemble(o, nb, w_a // hd)

    return [scores, outputs, normalise]


def _rwkv_stages(ctx, rs_ref, b0, nb, w_b, mk):
    c = CHUNK
    n_pairs = w_b // LANES
    lane_lo = mk.lane_lo
    pairs = [(b, pr) for b in range(nb) for pr in range(n_pairs)]
    tl = lambda t, pair: _tile(t, pair[0], pair[1], LANES)
    zero = jnp.zeros((c, LANES), BF16)
    lo = lambda t: jnp.where(lane_lo, t, zero)
    hi = lambda t: jnp.where(lane_lo, zero, t)
    cat0 = lambda *ts: jnp.concatenate(ts, axis=0)
    st = {}

    def state_read():
        at, rt, bt, kt, a0, r0, bl, kl, v = ctx['rw']
        st['s_old'] = {pp: rs_ref[b0 + pp[0], pp[1]] for pp in pairs}
        st['z'] = {pp: _dot_nt(cat0(tl(a0, pp), tl(r0, pp)), st['s_old'][pp].astype(BF16)) for pp in pairs}

    def gram():
        at, rt, bt, kt, a0, r0, bl, kl, v = ctx['rw']
        g = {}
        for pp in pairs:
            b_, k_ = tl(bt, pp), tl(kt, pp)
            g[pp] = jnp.where(
                mk.gmask, _dot_nt(cat0(tl(at, pp), tl(rt, pp)), cat0(lo(b_), lo(k_), hi(k_), hi(b_))), 0.0)
        st['g'] = g

    def rhs0():
        v = ctx['rw'][8]
        x, pw = {}, {}
        for pp in pairs:
            t0, t1 = st['g'][pp][:c, :LANES], st['g'][pp][:c, LANES:]
            pw[pp] = jnp.where(lane_lo, t0, t1).astype(BF16)
            lak = jnp.where(lane_lo, t1, t0).astype(BF16)
            x[pp] = st['z'][pp][:c] + _dot(lak, cat0(lo(tl(v, pp)), hi(tl(v, pp))))
        st['x'], st['pw'] = x, pw

    def solve_step():
        x, pw, res = st['x'], st['pw'], {}
        for pp in pairs:
            xb = x[pp].astype(BF16)
            q0 = jnp.where(lane_lo, pw[pp], xb)
            q1 = jnp.where(lane_lo, xb, pw[pp])
            rhs = cat0(jnp.concatenate([q0, zero], axis=1), jnp.concatenate([zero, q1], axis=1))
            res[pp] = _dot(pw[pp], rhs)
        st['x'] = {pp: x[pp] + jnp.where(lane_lo, res[pp][:, LANES:], res[pp][:, :LANES]) for pp in pairs}
        st['pw'] = {pp: jnp.where(lane_lo, res[pp][:, :LANES], res[pp][:, LANES:]).astype(BF16)
                    for pp in pairs}

    def solve_last():
        x, pw = st['x'], st['pw']
        for pp in pairs:
            xb = x[pp].astype(BF16)
            x[pp] = x[pp] + _dot(pw[pp], cat0(hi(xb), lo(xb)))

    def outputs():
        at, rt, bt, kt, a0, r0, bl, kl, v = ctx['rw']
        dl = ctx['rw_dl']
        y = {}
        for pp in pairs:
            ub, vv = st['x'][pp].astype(BF16), tl(v, pp)
            y[pp] = st['z'][pp][c:] + _dot(st['g'][pp][c:].astype(BF16), cat0(hi(ub), hi(vv), lo(vv), lo(ub)))
            upd = _dot_tn(cat0(ub, vv), cat0(tl(bl, pp), tl(kl, pp)))
            rs_ref[b0 + pp[0], pp[1]] = tl(dl, pp)[0:1] * st['s_old'][pp] + jnp.where(mk.admask, upd, 0.0)
        ctx['y'] = _assemble(y, nb, n_pairs)

    n_steps = 0
    n = 1
    while 2 * n < c:
        n_steps += 1
        n *= 2
    return [state_read, gram, rhs0] + [solve_step] * n_steps + [solve_last, outputs]


def _mixer_kernel(x_ref, mn_ref, win_ref, lbl_ref, hon_ref, mu_ref, w0_ref, a0_ref, wcomb_ref,
                  kk_ref, ka_ref, rk_ref, gnw_ref, gnb_ref, wout_ref, o_ref,
                  p_ref, hs_ref, rs_ref, carry_ref, *, nb, n_groups, w_a, w_b):
    c = CHUNK
    rows = nb * c
    d_model = x_ref.shape[-1]
    n_rkv = 3 * w_b + LORA_PAD

    @pl.when(pl.program_id(1) == 0)
    def _():
        hs_ref[...] = jnp.zeros_like(hs_ref)
        rs_ref[...] = jnp.zeros_like(rs_ref)
        carry_ref[...] = jnp.zeros_like(carry_ref)

    mk = _Masks(c)
    logits = lbl_ref[...]
    e = jnp.exp(logits - jnp.max(logits, axis=0, keepdims=True))
    lb = e[0:1] / jnp.sum(e, axis=0, keepdims=True)

    def in_proj_stages(grp):
        b0 = grp * nb
        prow = slice(grp * rows, (grp + 1) * rows)
        hn = {}

        def norm():
            x = x_ref[b0:b0 + nb].reshape(rows, d_model)
            hn['h'] = _rms_norm(x, mn_ref[...]).astype(BF16)

        def cols(lo, hi):
            def project():
                p_ref[prow, lo:hi] = _dot(hn['h'], win_ref[:, lo:hi])
            return project

        n_cols = 4 * w_a + n_rkv
        return [norm] + [cols(lo, min(lo + IN_PROJ_COL_TILE, n_cols))
                         for lo in range(0, n_cols, IN_PROJ_COL_TILE)]

    def prep_stages(grp, ctx):
        bctxs = [dict() for _ in range(nb)]
        per_batch = [prep_batch_stages(grp * nb + b, bctxs[b]) for b in range(nb)]

        def collect():
            rows_of = lambda name: jnp.concatenate([bc_[name] for bc_ in bctxs], axis=0)
            for name in ('hg_qt', 'hg_kt', 'hg_kl', 'hg_q0', 'hg_d', 'hg_v', 'bonus', 'gate', 'rw_dl'):
                ctx[name] = rows_of(name)
            ctx['rw'] = tuple(jnp.concatenate([bc_['rw'][i] for bc_ in bctxs], axis=0) for i in range(9))

        return [fn for stage in zip(*per_batch) for fn in stage] + [collect]

    def prep_batch_stages(bidx, ctx):
        prow = slice(bidx * c, (bidx + 1) * c)

        def hgrn_gates():
            q_a = p_ref[prow, 0:w_a]
            f_a = p_ref[prow, w_a:2 * w_a]
            forget = lb + (1.0 - lb) * _sigmoid(f_a)
            ctx['hg_logf'] = _split_bf16(jnp.log(forget))
            ctx['hg_kh'] = 1.0 - forget
            ctx['hg_q'] = q_a * _sigmoid(q_a)

        def hgrn_decays():
            hi, lo = ctx['hg_logf']
            bc = _dot(mk.tri, hi) + _dot(mk.tri, lo)
            q, kh = ctx['hg_q'], ctx['hg_kh']
            b_ref = _block_rows(bc, c // 2)
            qt = q * jnp.exp(bc - b_ref)
            kt = kh * jnp.exp(b_ref - bc)
            ctx['hg_qt'] = qt.astype(BF16)
            ctx['hg_kt'] = kt.astype(BF16)
            ctx['hg_kl'] = (kt * _block_rows(bc, c - 1, lambda last, ref: jnp.exp(last - ref))).astype(BF16)
            ctx['hg_q0'] = (qt * _block_rows(bc, c - 1, lambda last, ref: jnp.exp(ref))).astype(BF16)
            ctx['hg_d'] = _block_rows(bc, c - 1, lambda last, ref: jnp.exp(last))
            ctx['hg_v'] = p_ref[prow, 2 * w_a:3 * w_a].astype(BF16)

        def rwkv_shift():
            pb = p_ref[prow, 4 * w_a:4 * w_a + n_rkv]
            prev = jnp.broadcast_to(carry_ref[bidx:bidx + 1, :], (c, n_rkv))
            shifted = jnp.where(mk.row0, prev, pltpu.roll(pb, 1, 0))
            carry_ref[bidx:bidx + 1, :] = pb[c - 1:c, :]
            pb = pb + mu_ref[...] * (shifted - pb)
            ctx['r'] = pb[:, 0:w_b]
            ctx['k'] = pb[:, w_b:2 * w_b]
            ctx['v'] = pb[:, 2 * w_b:3 * w_b]
            low = pb[:, 3 * w_b:]
            ll = mk.lora_lane
            act = jnp.where(ll < DECAY_LORA, jnp.tanh(low),
                            jnp.where(ll < DECAY_LORA + AAA_LORA, low,
                                      jnp.where(ll < DECAY_LORA + AAA_LORA + GATE_LORA, _sigmoid(low), 0.0)))
            ctx['act'] = act.astype(BF16)

        def rwkv_lora():
            r, k = ctx['r'], ctx['k']
            ld = _dot(ctx['act'], wcomb_ref[...])
            w_log = -_softplus(-(w0_ref[...] + ld[:, 0:w_b])) - 0.5
            lw = -jnp.exp(w_log)
            a = _sigmoid(a0_ref[...] + ld[:, w_b:2 * w_b])
            ctx['gate'] = ld[:, 2 * w_b:3 * w_b]
            kk = k * kk_ref[...]
            k = k * (1.0 + (a - 1.0) * ka_ref[...])
            ctx['k'], ctx['a'], ctx['kk'], ctx['lw'] = k, a, kk, lw
            ctx['lw_split'] = _split_bf16(lw)
            ctx['kk_sq'] = kk * kk
            ctx['rk'] = r * k * rk_ref[...]

        def rwkv_sums():
            hi, lo = ctx['lw_split']
            ctx['gc'] = _dot(mk.tri, hi) + _dot(mk.tri, lo)
            kk = ctx['kk'] / jnp.maximum(jnp.sqrt(_segsum(ctx['kk_sq'], mk.lane_lo)), L2_EPS)
            ctx['bonus'] = _segsum(ctx['rk'], mk.lane_lo) * ctx['v']
            ctx['av'], ctx['bv'] = -kk, kk * ctx['a']

        def rwkv_decays():
            r, k, v, av, bv, lw, gc = (ctx[n] for n in ('r', 'k', 'v', 'av', 'bv', 'lw', 'gc'))
            gref = _block_rows(gc, c // 2)
            e_up = jnp.exp(gc - gref)
            e_dn = jnp.exp(gref - gc)
            at, rt, bt, kt = av * jnp.exp(gc - lw - gref), r * e_up, bv * e_dn, k * e_dn
            s_ref = _block_rows(gc, c - 1, lambda last, ref: jnp.exp(ref))
            s_last = _block_rows(gc, c - 1, lambda last, ref: jnp.exp(last - ref))
            full = (at, rt, bt, kt, at * s_ref, rt * s_ref, bt * s_last, kt * s_last, _swap_halves(v))
            ctx['rw'] = tuple(t.astype(BF16) for t in full)
            ctx['rw_dl'] = _block_rows(gc, c - 1, lambda last, ref: jnp.exp(last))

        return [hgrn_gates, hgrn_decays, rwkv_shift, rwkv_lora, rwkv_sums, rwkv_decays]

    def chain_stages(grp, ctx):
        b0 = grp * nb
        return _hgrn_stages(ctx, hs_ref, b0, nb, w_a, mk) + _rwkv_stages(ctx, rs_ref, b0, nb, w_b, mk)

    def post_stages(grp, ctx):
        b0 = grp * nb
        inv_n = 1.0 / HB_HEAD_DIM
        parts = [dict() for _ in range(nb)]

        def batch_stages(b, st):
            brow = slice(b * c, (b + 1) * c)
            prow = slice((b0 + b) * c, (b0 + b + 1) * c)

            def gate_a():
                g_a = p_ref[prow, 3 * w_a:4 * w_a]
                st['o_a'] = (ctx['o_a'][brow] * hon_ref[...] * (g_a * _sigmoid(g_a))).astype(BF16)
                st['y'] = _swap_halves(ctx['y'][brow])

            def gn_mean():
                yc = st['y'] - _segsum(st['y'], mk.lane_lo) * inv_n
                st['yc'], st['yc_sq'] = yc, yc * yc

            def gn_var():
                var = _segsum(st['yc_sq'], mk.lane_lo) * inv_n
                yn = st['yc'] * lax.rsqrt(var + RWKV_GN_EPS) * gnw_ref[...] + gnb_ref[...]
                o_b = ((yn + ctx['bonus'][brow]) * ctx['gate'][brow]).astype(BF16)
                st['o'] = jnp.concatenate([st['o_a'], o_b], axis=1)

            return [gate_a, gn_mean, gn_var]

        def out_proj():
            o = jnp.concatenate([st['o'] for st in parts], axis=0)
            x = x_ref[b0:b0 + nb].reshape(rows, d_model)
            o_ref[b0:b0 + nb] = (x + _dot(o, wout_ref[...])).reshape(nb, c, d_model)

        per_batch = [batch_stages(b, parts[b]) for b in range(nb)]
        return [fn for stage in zip(*per_batch) for fn in stage] + [out_proj]

    ctxs = [dict() for _ in range(n_groups)]
    _merge_stages(in_proj_stages(0))
    lists = [prep_stages(0, ctxs[0])]
    if n_groups > 1:
        lists.append(in_proj_stages(1))
    _merge_stages(*lists)
    for grp in range(n_groups):
        lists = [chain_stages(grp, ctxs[grp])]
        if grp + 1 < n_groups:
            lists.append(prep_stages(grp + 1, ctxs[grp + 1]))
        if grp + 2 < n_groups:
            lists.append(in_proj_stages(grp + 2))
        if grp >= 1:
            lists.append(post_stages(grp - 1, ctxs[grp - 1]))
        _merge_stages(*lists)
    _merge_stages(post_stages(n_groups - 1, ctxs[n_groups - 1]))


def _mixer(x, mix_norm, w_in, lb_logits, hgrn_out_norm, mu, w0, w2, a0, a2, g2, k_k, k_a, r_k,
           gn_w, gn_b, w_out):
    bsz, t, d = x.shape
    w_a = hgrn_out_norm.shape[-1]
    w_b = w0.shape[-1]
    nb, n_groups, c = MIXER_GROUP, MIXER_GROUPS_PER_STEP, CHUNK
    nbs = nb * n_groups
    n_lora = DECAY_LORA + AAA_LORA + GATE_LORA
    assert bsz % nbs == 0 and t % c == 0
    assert w_a % HA_HEAD_DIM == 0 and w_b % LANES == 0
    assert w_in.shape[1] == 4 * w_a + 3 * w_b + n_lora
    pad = LORA_PAD - n_lora
    n_cols = 4 * w_a + 3 * w_b + LORA_PAD
    n_rkv = 3 * w_b + LORA_PAD
    w_in_p = jnp.pad(w_in, ((0, 0), (0, pad))).astype(BF16)
    mu_p = jnp.pad(mu.reshape(1, -1), ((0, 0), (0, pad)))
    wcomb = jnp.zeros((LORA_PAD, 3 * w_b), F32)
    wcomb = wcomb.at[0:DECAY_LORA, 0:w_b].set(w2)
    wcomb = wcomb.at[DECAY_LORA:DECAY_LORA + AAA_LORA, w_b:2 * w_b].set(a2)
    wcomb = wcomb.at[DECAY_LORA + AAA_LORA:n_lora, 2 * w_b:3 * w_b].set(g2)
    row = lambda vec: vec.reshape(1, -1).astype(F32)
    n_pairs = w_b // LANES
    n_heads_a = w_a // HA_HEAD_DIM
    state_bytes = nbs * (n_heads_a * HA_HEAD_DIM * HA_HEAD_DIM + n_pairs * LANES * LANES) * 4
    weight_bytes = (d * n_cols + LORA_PAD * 3 * w_b + (w_a + w_b) * d) * 2
    vmem = weight_bytes + state_bytes + 3 * nbs * c * n_cols * 4 + 4 * nbs * c * d * 4 + 12 * MIB
    return pl.pallas_call(
        functools.partial(_mixer_kernel, nb=nb, n_groups=n_groups, w_a=w_a, w_b=w_b),
        grid=(bsz // nbs, t // c),
        in_specs=[
            pl.BlockSpec((nbs, c, d), lambda i, j: (i, j, 0)),
            _resident((1, d)),
            _resident((d, n_cols)),
            _resident(lb_logits.shape),
            _resident((1, w_a)),
            _resident((1, n_rkv)),
            _resident((1, w_b)),
            _resident((1, w_b)),
            _resident((LORA_PAD, 3 * w_b)),
            _resident((1, w_b)),
            _resident((1, w_b)),
            _resident((1, w_b)),
            _resident((1, w_b)),
            _resident((1, w_b)),
            _resident((w_a + w_b, d)),
        ],
        out_specs=pl.BlockSpec((nbs, c, d), lambda i, j: (i, j, 0)),
        out_shape=jax.ShapeDtypeStruct((bsz, t, d), F32),
        scratch_shapes=[
            pltpu.VMEM((nbs * c, n_cols), F32),
            pltpu.VMEM((nbs, n_heads_a, HA_HEAD_DIM, HA_HEAD_DIM), F32),
            pltpu.VMEM((nbs, n_pairs, LANES, LANES), F32),
            pltpu.VMEM((nbs, n_rkv), F32),
        ],
        compiler_params=pltpu.CompilerParams(
            dimension_semantics=("arbitrary", "arbitrary"), vmem_limit_bytes=vmem),
        name="mixer",
    )(x, row(mix_norm), w_in_p, lb_logits.astype(F32), row(hgrn_out_norm), mu_p, row(w0), row(a0),
      wcomb.astype(BF16), row(k_k), row(k_a), row(r_k), row(gn_w), row(gn_b), w_out.astype(BF16))


def kernel(x, ffn1_norm, ffn1_w_gate, ffn1_w_up, ffn1_w_down, mix_norm, w_in, hgrn_lb_logits, hgrn_out_norm, rwkv_shift_mu, rwkv_w0, rwkv_w2, rwkv_a0, rwkv_a2, rwkv_g2, rwkv_k_k, rwkv_k_a, rwkv_r_k, rwkv_gn_w, rwkv_gn_b, w_out, ffn2_norm, ffn2_w_gate, ffn2_w_up, ffn2_w_down, final_norm):
    bsz, t, d = x.shape
    depth = ffn1_norm.shape[0]
    assert depth == 1 and hgrn_lb_logits.shape[0] == depth + 1
    l = 0
    row = lambda vec: vec.reshape(1, -1).astype(F32)
    fg = row(final_norm)
    h = _ffn(x.reshape(bsz * t, d), row(ffn1_norm[l]), ffn1_w_gate[l].astype(BF16),
             ffn1_w_up[l].astype(BF16), ffn1_w_down[l].astype(BF16), fg, final_norm=False)
    h = _mixer(h.reshape(bsz, t, d), mix_norm[l], w_in[l], hgrn_lb_logits, hgrn_out_norm[l],
               rwkv_shift_mu[l], rwkv_w0[l], rwkv_w2[l], rwkv_a0[l], rwkv_a2[l], rwkv_g2[l],
               rwkv_k_k[l], rwkv_k_a[l], rwkv_r_k[l], rwkv_gn_w[l], rwkv_gn_b[l], w_out[l])
    h = _ffn(h.reshape(bsz * t, d), row(ffn2_norm[l]), ffn2_w_gate[l].astype(BF16),
             ffn2_w_up[l].astype(BF16), ffn2_w_down[l].astype(BF16), fg, final_norm=True)
    return h.reshape(bsz, t, d)
```

```python
import functools
import math

import jax
import jax.numpy as jnp
from jax import lax
from jax.experimental import pallas as pl
from jax.experimental.pallas import tpu as pltpu

F32 = jnp.float32
BF16 = jnp.bfloat16

NORM_EPS = 1e-6
RWKV_GN_EPS = 64e-5
L2_EPS = 1e-12
DECAY_SCALE = math.exp(-0.5)

LANES = 128
MXU_COLS = 256
HA_HEAD_DIM = 128
HB_HEAD_DIM = 64
CHUNK = 64
DECAY_LORA, AAA_LORA, GATE_LORA = 32, 32, 96
LORA_PAD = 256
FFN_ROW_TILE = 512
FFN_COL_TILE = 768
MIXER_GROUP = 4
MIXER_GROUPS_PER_STEP = 2
IN_PROJ_COL_TILE = 512
MIB = 1024 * 1024


def _dot(a, b):
    return jnp.dot(a, b, preferred_element_type=F32)


def _dot_nt(a, b):
    return lax.dot_general(a, b, (((1,), (1,)), ((), ())), preferred_element_type=F32)


def _dot_tn(a, b):
    return lax.dot_general(a, b, (((0,), (0,)), ((), ())), preferred_element_type=F32)


def _split_bf16(x):
    hi = x.astype(BF16)
    lo = (x - hi.astype(F32)).astype(BF16)
    return hi, lo


def _rms_norm(x, g):
    return x * lax.rsqrt(jnp.mean(x * x, axis=-1, keepdims=True) + NORM_EPS) * g


def _sigmoid(x):
    return 0.5 * jnp.tanh(0.5 * x) + 0.5


def _ffn_kernel(x_ref, g_ref, wg_ref, wu_ref, wd_ref, fn_ref, o_ref, *, col_tiles, final_norm):
    x = x_ref[...]
    h = _rms_norm(x, g_ref[...]).astype(BF16)
    acc = jnp.zeros(x.shape, F32)
    for lo, hi in col_tiles:
        gate = _dot(h, wg_ref[:, lo:hi])
        up = _dot(h, wu_ref[:, lo:hi])
        act = (gate * _sigmoid(gate) * up).astype(BF16)
        acc = acc + _dot(act, wd_ref[lo:hi, :])
    y = x + 0.5 * acc
    if final_norm:
        y = _rms_norm(y, fn_ref[...])
    o_ref[...] = y


def _resident(shape):
    return pl.BlockSpec(shape, lambda *_: (0,) * len(shape), pipeline_mode=pl.Buffered(1))


def _ffn(x2d, norm_g, w_gate, w_up, w_down, final_g, *, final_norm):
    n, d = x2d.shape
    d_ff = w_gate.shape[1]
    tm = FFN_ROW_TILE
    assert n % tm == 0 and d_ff % MXU_COLS == 0
    col_tiles = tuple((lo, min(lo + FFN_COL_TILE, d_ff)) for lo in range(0, d_ff, FFN_COL_TILE))
    weight_bytes = 3 * d * d_ff * 2
    tile_bytes = tm * d * 4
    vmem = weight_bytes + 8 * tile_bytes + 3 * tm * FFN_COL_TILE * 4 + 8 * MIB
    return pl.pallas_call(
        functools.partial(_ffn_kernel, col_tiles=col_tiles, final_norm=final_norm),
        grid=(n // tm,),
        in_specs=[
            pl.BlockSpec((tm, d), lambda i: (i, 0)),
            _resident((1, d)),
            _resident((d, d_ff)),
            _resident((d, d_ff)),
            _resident((d_ff, d)),
            _resident((1, d)),
        ],
        out_specs=pl.BlockSpec((tm, d), lambda i: (i, 0)),
        out_shape=jax.ShapeDtypeStruct((n, d), F32),
        compiler_params=pltpu.CompilerParams(
            dimension_semantics=("arbitrary",), vmem_limit_bytes=vmem),
        name="ffn_final" if final_norm else "ffn",
    )(x2d, norm_g, w_gate, w_up, w_down, final_g)


def _iota2(shape, axis):
    return lax.broadcasted_iota(jnp.int32, shape, axis)


def _segsum(x, lane_lo):
    outs = []
    for j in range(x.shape[1] // LANES):
        t = x[:, j * LANES:(j + 1) * LANES]
        lo = jnp.sum(jnp.where(lane_lo, t, 0.0), axis=-1, keepdims=True)
        hi = jnp.sum(jnp.where(lane_lo, 0.0, t), axis=-1, keepdims=True)
        outs.append(jnp.where(lane_lo, lo, hi))
    return jnp.concatenate(outs, axis=1)


def _block_rows(x, idx, fn=None):
    c = CHUNK
    outs = []
    for r in range(0, x.shape[0], c):
        row = x[r + idx:r + idx + 1]
        if fn is not None:
            row = fn(row, x[r + c // 2:r + c // 2 + 1])
        outs.append(jnp.broadcast_to(row, (c, x.shape[1])))
    return jnp.concatenate(outs, axis=0)


def _tile(x, b, j, width):
    return x[b * CHUNK:(b + 1) * CHUNK, j * width:(j + 1) * width]


def _assemble(tiles, nb, nj):
    return jnp.concatenate(
        [jnp.concatenate([tiles[(b, j)] for j in range(nj)], axis=1) for b in range(nb)], axis=0)


def _merge_stages(*stage_lists):
    tagged = []
    for stages in stage_lists:
        tagged += [((i + 0.5) / len(stages), i, fn) for i, fn in enumerate(stages)]
    for _, _, fn in sorted(tagged, key=lambda t: t[:2]):
        fn()


class _Masks:
    def __init__(self, rows):
        c = CHUNK
        ri, ci = _iota2((c, c), 0), _iota2((c, c), 1)
        self.causal = ci <= ri
        bi, bj = _iota2((rows, rows), 0), _iota2((rows, rows), 1)
        same_block = lax.shift_right_logical(bi, 6) == lax.shift_right_logical(bj, 6)
        self.tri = ((bj <= bi) & same_block).astype(BF16)
        gi, gj = _iota2((2 * c, 4 * c), 0), _iota2((2 * c, 4 * c), 1)
        ti, sj = gi & (c - 1), gj & (c - 1)
        self.gmask = (sj < ti) | ((gi >= c) & (sj == ti))
        li, lj = _iota2((LANES, LANES), 0), _iota2((LANES, LANES), 1)
        self.bdmask = lax.shift_right_logical(li, 6) == lax.shift_right_logical(lj, 6)
        self.lane_lo = _iota2((1, LANES), 1) < HB_HEAD_DIM
        self.row0 = (_iota2((rows, 1), 0) & (c - 1)) == 0
        self.lora_lane = _iota2((1, LORA_PAD), 1)


def _hgrn_stages(ctx, hs_ref, b0, nb, w_a, mk):
    hd = HA_HEAD_DIM
    chains = [(b, h) for b in range(nb) for h in range(w_a // hd)]
    st = {}

    def scores():
        qt, kt = ctx['hg_qt'], ctx['hg_kt']
        st['s_old'] = {ch: hs_ref[b0 + ch[0], ch[1]] for ch in chains}
        st['sc'] = {ch: jnp.where(mk.causal, _dot_nt(_tile(qt, *ch, hd), _tile(kt, *ch, hd)), 0.0).astype(BF16)
                    for ch in chains}

    def outputs():
        v, q0, kl, d = ctx['hg_v'], ctx['hg_q0'], ctx['hg_kl'], ctx['hg_d']
        s_old = st['s_old']
        o = {ch: _dot(st['sc'][ch], _tile(v, *ch, hd)) + _dot_nt(_tile(q0, *ch, hd), s_old[ch].astype(BF16))
             for ch in chains}
        for ch in chains:
            hs_ref[b0 + ch[0], ch[1]] = (_tile(d, *ch, hd)[0:1] * s_old[ch]
                                         + _dot_tn(_tile(v, *ch, hd), _tile(kl, *ch, hd)))
        st['o'] = o

    def normalise():
        o = {ch: t * lax.rsqrt(jnp.mean(t * t, axis=-1, keepdims=True) + NORM_EPS) for ch, t in st['o'].items()}
        ctx['o_a'] = _assemble(o, nb, w_a // hd)

    return [scores, outputs, normalise]


def _rwkv_stages(ctx, rs_ref, b0, nb, w_b, mk):
    c = CHUNK
    n_pairs = w_b // LANES
    lane_lo = mk.lane_lo
    pairs = [(b, pr) for b in range(nb) for pr in range(n_pairs)]
    tl = lambda t, pair: _tile(t, pair[0], pair[1], LANES)
    zero = jnp.zeros((c, LANES), BF16)
    lo = lambda t: jnp.where(lane_lo, t, zero)
    hi = lambda t: jnp.where(lane_lo, zero, t)
    cat0 = lambda *ts: jnp.concatenate(ts, axis=0)
    st = {}

    def state_read():
        at, rt, bt, kt, a0, r0, bl, kl, v = ctx['rw']
        st['s_old'] = {pp: rs_ref[b0 + pp[0], pp[1]] for pp in pairs}
        st['z'] = {pp: _dot_nt(cat0(tl(a0, pp), tl(r0, pp)), st['s_old'][pp].astype(BF16)) for pp in pairs}

    def gram():
        at, rt, bt, kt, a0, r0, bl, kl, v = ctx['rw']
        g = {}
        for pp in pairs:
            b_, k_ = tl(bt, pp), tl(kt, pp)
            g[pp] = jnp.where(
                mk.gmask, _dot_nt(cat0(tl(at, pp), tl(rt, pp)), cat0(lo(b_), hi(b_), lo(k_), hi(k_))), 0.0)
        st['g'] = g

    def rhs0():
        v = ctx['rw'][8]
        x, pw = {}, {}
        for pp in pairs:
            pw[pp] = st['g'][pp][:c, :LANES].astype(BF16)
            lak = st['g'][pp][:c, LANES:].astype(BF16)
            x[pp] = st['z'][pp][:c] + _dot(lak, cat0(lo(tl(v, pp)), hi(tl(v, pp))))
        st['x'], st['pw'] = x, pw

    def solve_step():
        x, pw, res = st['x'], st['pw'], {}
        for pp in pairs:
            xb = x[pp].astype(BF16)
            rhs = jnp.concatenate([cat0(lo(xb), hi(xb)), cat0(lo(pw[pp]), hi(pw[pp]))], axis=1)
            res[pp] = _dot(pw[pp], rhs)
        st['x'] = {pp: x[pp] + res[pp][:, :LANES] for pp in pairs}
        st['pw'] = {pp: res[pp][:, LANES:].astype(BF16) for pp in pairs}

    def solve_last():
        x, pw = st['x'], st['pw']
        for pp in pairs:
            xb = x[pp].astype(BF16)
            x[pp] = x[pp] + _dot(pw[pp], cat0(lo(xb), hi(xb)))

    def outputs():
        at, rt, bt, kt, a0, r0, bl, kl, v = ctx['rw']
        dl = ctx['rw_dl']
        y = {}
        for pp in pairs:
            ub, vv = st['x'][pp].astype(BF16), tl(v, pp)
            y[pp] = st['z'][pp][c:] + _dot(st['g'][pp][c:].astype(BF16), cat0(lo(ub), hi(ub), lo(vv), hi(vv)))
            upd = _dot_tn(cat0(ub, vv), cat0(tl(bl, pp), tl(kl, pp)))
            rs_ref[b0 + pp[0], pp[1]] = tl(dl, pp)[0:1] * st['s_old'][pp] + jnp.where(mk.bdmask, upd, 0.0)
        ctx['y'] = _assemble(y, nb, n_pairs)

    n_steps = 0
    n = 1
    while 2 * n < c:
        n_steps += 1
        n *= 2
    return [state_read, gram, rhs0] + [solve_step] * n_steps + [solve_last, outputs]


def _mixer_kernel(x_ref, xn_ref, mn_ref, win_ref, lbl_ref, hon_ref, mu_ref, w0_ref, a0_ref, wcomb_ref,
                  kk_ref, ka_ref, rk_ref, gnw_ref, gnb_ref, wout_ref, o_ref,
                  p_ref, hs_ref, rs_ref, carry_ref, *, nb, n_groups, w_a, w_b):
    c = CHUNK
    rows = nb * c
    d_model = x_ref.shape[-1]
    n_rkv = 3 * w_b + LORA_PAD

    def in_proj_stages(grp, load_x):
        prow = slice(grp * rows, (grp + 1) * rows)
        hn = {}

        def norm():
            x = load_x().reshape(rows, d_model)
            hn['h'] = _rms_norm(x, mn_ref[...]).astype(BF16)

        def cols(lo, hi):
            def project():
                p_ref[prow, lo:hi] = _dot(hn['h'], win_ref[:, lo:hi])
            return project

        n_cols = 4 * w_a + n_rkv
        return [norm] + [cols(lo, min(lo + IN_PROJ_COL_TILE, n_cols))
                         for lo in range(0, n_cols, IN_PROJ_COL_TILE)]

    group_x = lambda grp: (lambda: x_ref[grp * nb:(grp + 1) * nb])

    @pl.when(pl.program_id(1) == 0)
    def _():
        hs_ref[...] = jnp.zeros_like(hs_ref)
        rs_ref[...] = jnp.zeros_like(rs_ref)
        carry_ref[...] = jnp.zeros_like(carry_ref)
        _merge_stages(in_proj_stages(0, group_x(0)))

    mk = _Masks(c)
    logits = lbl_ref[...]
    e = jnp.exp(logits - jnp.max(logits, axis=0, keepdims=True))
    lb = e[0:1] / jnp.sum(e, axis=0, keepdims=True)

    def prep_stages(grp, ctx):
        bctxs = [dict() for _ in range(nb)]
        per_batch = [prep_batch_stages(grp * nb + b, bctxs[b]) for b in range(nb)]

        def collect():
            rows_of = lambda name: jnp.concatenate([bc_[name] for bc_ in bctxs], axis=0)
            for name in ('hg_qt', 'hg_kt', 'hg_kl', 'hg_q0', 'hg_d', 'hg_v', 'bonus', 'gate', 'rw_dl'):
                ctx[name] = rows_of(name)
            ctx['rw'] = tuple(jnp.concatenate([bc_['rw'][i] for bc_ in bctxs], axis=0) for i in range(9))

        return [fn for stage in zip(*per_batch) for fn in stage] + [collect]

    def prep_batch_stages(bidx, ctx):
        prow = slice(bidx * c, (bidx + 1) * c)

        def hgrn_gates():
            q_a = p_ref[prow, 0:w_a]
            f_a = p_ref[prow, w_a:2 * w_a]
            forget = lb + (1.0 - lb) * _sigmoid(f_a)
            ctx['hg_logf'] = _split_bf16(jnp.log(forget))
            ctx['hg_kh'] = 1.0 - forget
            ctx['hg_q'] = q_a * _sigmoid(q_a)

        def hgrn_decays():
            hi, lo = ctx['hg_logf']
            bc = _dot(mk.tri, hi) + _dot(mk.tri, lo)
            q, kh = ctx['hg_q'], ctx['hg_kh']
            b_ref = _block_rows(bc, c // 2)
            qt = q * jnp.exp(bc - b_ref)
            kt = kh * jnp.exp(b_ref - bc)
            ctx['hg_qt'] = qt.astype(BF16)
            ctx['hg_kt'] = kt.astype(BF16)
            ctx['hg_kl'] = (kt * _block_rows(bc, c - 1, lambda last, ref: jnp.exp(last - ref))).astype(BF16)
            ctx['hg_q0'] = (qt * _block_rows(bc, c - 1, lambda last, ref: jnp.exp(ref))).astype(BF16)
            ctx['hg_d'] = _block_rows(bc, c - 1, lambda last, ref: jnp.exp(last))
            ctx['hg_v'] = p_ref[prow, 2 * w_a:3 * w_a].astype(BF16)

        def rwkv_shift():
            pb = p_ref[prow, 4 * w_a:4 * w_a + n_rkv]
            prev = jnp.broadcast_to(carry_ref[bidx:bidx + 1, :], (c, n_rkv))
            shifted = jnp.where(mk.row0, prev, pltpu.roll(pb, 1, 0))
            carry_ref[bidx:bidx + 1, :] = pb[c - 1:c, :]
            pb = pb + mu_ref[...] * (shifted - pb)
            ctx['r'] = pb[:, 0:w_b]
            ctx['k'] = pb[:, w_b:2 * w_b]
            ctx['v'] = pb[:, 2 * w_b:3 * w_b]
            low = pb[:, 3 * w_b:]
            ll = mk.lora_lane
            act = jnp.where(ll < DECAY_LORA, jnp.tanh(low),
                            jnp.where(ll < DECAY_LORA + AAA_LORA, low,
                                      jnp.where(ll < DECAY_LORA + AAA_LORA + GATE_LORA, _sigmoid(low), 0.0)))
            ctx['act'] = act.astype(BF16)

        def rwkv_lora():
            r, k = ctx['r'], ctx['k']
            ld = _dot(ctx['act'], wcomb_ref[...])
            lw = -DECAY_SCALE * _sigmoid(w0_ref[...] + ld[:, 0:w_b])
            a = _sigmoid(a0_ref[...] + ld[:, w_b:2 * w_b])
            ctx['gate'] = ld[:, 2 * w_b:3 * w_b]
            kk = k * kk_ref[...]
            k = k * (1.0 + (a - 1.0) * ka_ref[...])
            ctx['k'], ctx['a'], ctx['kk'], ctx['lw'] = k, a, kk, lw
            ctx['lw_split'] = _split_bf16(lw)
            ctx['kk_sq'] = kk * kk
            ctx['rk'] = r * k * rk_ref[...]

        def rwkv_sums():
            hi, lo = ctx['lw_split']
            ctx['gc'] = _dot(mk.tri, hi) + _dot(mk.tri, lo)
            kk = ctx['kk'] * lax.rsqrt(jnp.maximum(_segsum(ctx['kk_sq'], mk.lane_lo), L2_EPS * L2_EPS))
            ctx['bonus'] = _segsum(ctx['rk'], mk.lane_lo) * ctx['v']
            ctx['av'], ctx['bv'] = -kk, kk * ctx['a']

        def rwkv_decays():
            r, k, v, av, bv, lw, gc = (ctx[n] for n in ('r', 'k', 'v', 'av', 'bv', 'lw', 'gc'))
            gref = _block_rows(gc, c // 2)
            e_up = jnp.exp(gc - gref)
            e_dn = jnp.exp(gref - gc)
            at, rt, bt, kt = av * jnp.exp(gc - lw - gref), r * e_up, bv * e_dn, k * e_dn
            s_ref = _block_rows(gc, c - 1, lambda last, ref: jnp.exp(ref))
            s_last = _block_rows(gc, c - 1, lambda last, ref: jnp.exp(last - ref))
            full = (at, rt, bt, kt, at * s_ref, rt * s_ref, bt * s_last, kt * s_last, v)
            ctx['rw'] = tuple(t.astype(BF16) for t in full)
            ctx['rw_dl'] = _block_rows(gc, c - 1, lambda last, ref: jnp.exp(last))

        return [hgrn_gates, hgrn_decays, rwkv_shift, rwkv_lora, rwkv_sums, rwkv_decays]

    def chain_stages(grp, ctx):
        b0 = grp * nb
        return _hgrn_stages(ctx, hs_ref, b0, nb, w_a, mk) + _rwkv_stages(ctx, rs_ref, b0, nb, w_b, mk)

    def post_stages(grp, ctx):
        b0 = grp * nb
        inv_n = 1.0 / HB_HEAD_DIM
        parts = [dict() for _ in range(nb)]

        def batch_stages(b, st):
            brow = slice(b * c, (b + 1) * c)
            prow = slice((b0 + b) * c, (b0 + b + 1) * c)

            def gate_a():
                g_a = p_ref[prow, 3 * w_a:4 * w_a]
                st['o_a'] = (ctx['o_a'][brow] * hon_ref[...] * (g_a * _sigmoid(g_a))).astype(BF16)
                st['y'] = ctx['y'][brow]

            def gn_mean():
                yc = st['y'] - _segsum(st['y'], mk.lane_lo) * inv_n
                st['yc'], st['yc_sq'] = yc, yc * yc

            def gn_var():
                var = _segsum(st['yc_sq'], mk.lane_lo) * inv_n
                yn = st['yc'] * lax.rsqrt(var + RWKV_GN_EPS) * gnw_ref[...] + gnb_ref[...]
                o_b = ((yn + ctx['bonus'][brow]) * ctx['gate'][brow]).astype(BF16)
                st['o'] = jnp.concatenate([st['o_a'], o_b], axis=1)

            return [gate_a, gn_mean, gn_var]

        def out_proj():
            o = jnp.concatenate([st['o'] for st in parts], axis=0)
            x = x_ref[b0:b0 + nb].reshape(rows, d_model)
            o_ref[b0:b0 + nb] = (x + _dot(o, wout_ref[...])).reshape(nb, c, d_model)

        per_batch = [batch_stages(b, parts[b]) for b in range(nb)]
        return [fn for stage in zip(*per_batch) for fn in stage] + [out_proj]

    ctxs = [dict() for _ in range(n_groups)]
    lists = [prep_stages(0, ctxs[0])]
    if n_groups > 1:
        lists.append(in_proj_stages(1, group_x(1)))
    _merge_stages(*lists)
    for grp in range(n_groups):
        lists = [chain_stages(grp, ctxs[grp])]
        if grp + 1 < n_groups:
            lists.append(prep_stages(grp + 1, ctxs[grp + 1]))
        if grp + 2 < n_groups:
            lists.append(in_proj_stages(grp + 2, group_x(grp + 2)))
        if grp >= 1:
            lists.append(post_stages(grp - 1, ctxs[grp - 1]))
        _merge_stages(*lists)
    _merge_stages(post_stages(n_groups - 1, ctxs[n_groups - 1]),
                  in_proj_stages(0, lambda: xn_ref[...]))


def _mixer(x, mix_norm, w_in, lb_logits, hgrn_out_norm, mu, w0, w2, a0, a2, g2, k_k, k_a, r_k,
           gn_w, gn_b, w_out):
    bsz, t, d = x.shape
    w_a = hgrn_out_norm.shape[-1]
    w_b = w0.shape[-1]
    nb, n_groups, c = MIXER_GROUP, MIXER_GROUPS_PER_STEP, CHUNK
    nbs = nb * n_groups
    n_lora = DECAY_LORA + AAA_LORA + GATE_LORA
    assert bsz % nbs == 0 and t % c == 0
    assert w_a % HA_HEAD_DIM == 0 and w_b % LANES == 0
    assert w_in.shape[1] == 4 * w_a + 3 * w_b + n_lora
    pad = LORA_PAD - n_lora
    n_cols = 4 * w_a + 3 * w_b + LORA_PAD
    n_rkv = 3 * w_b + LORA_PAD
    w_in_p = jnp.pad(w_in, ((0, 0), (0, pad))).astype(BF16)
    mu_p = jnp.pad(mu.reshape(1, -1), ((0, 0), (0, pad)))
    wcomb = jnp.zeros((LORA_PAD, 3 * w_b), F32)
    wcomb = wcomb.at[0:DECAY_LORA, 0:w_b].set(w2)
    wcomb = wcomb.at[DECAY_LORA:DECAY_LORA + AAA_LORA, w_b:2 * w_b].set(a2)
    wcomb = wcomb.at[DECAY_LORA + AAA_LORA:n_lora, 2 * w_b:3 * w_b].set(g2)
    row = lambda vec: vec.reshape(1, -1).astype(F32)
    n_pairs = w_b // LANES
    n_heads_a = w_a // HA_HEAD_DIM
    state_bytes = nbs * (n_heads_a * HA_HEAD_DIM * HA_HEAD_DIM + n_pairs * LANES * LANES) * 4
    weight_bytes = (d * n_cols + LORA_PAD * 3 * w_b + (w_a + w_b) * d) * 2
    vmem = weight_bytes + state_bytes + 3 * nbs * c * n_cols * 4 + 4 * nbs * c * d * 4 + 12 * MIB
    return pl.pallas_call(
        functools.partial(_mixer_kernel, nb=nb, n_groups=n_groups, w_a=w_a, w_b=w_b),
        grid=(bsz // nbs, t // c),
        in_specs=[
            pl.BlockSpec((nbs, c, d), lambda i, j: (i, j, 0)),
            pl.BlockSpec((nb, c, d), lambda i, j: (i * n_groups, jnp.minimum(j + 1, t // c - 1), 0)),
            _resident((1, d)),
            _resident((d, n_cols)),
            _resident(lb_logits.shape),
            _resident((1, w_a)),
            _resident((1, n_rkv)),
            _resident((1, w_b)),
            _resident((1, w_b)),
            _resident((LORA_PAD, 3 * w_b)),
            _resident((1, w_b)),
            _resident((1, w_b)),
            _resident((1, w_b)),
            _resident((1, w_b)),
            _resident((1, w_b)),
            _resident((w_a + w_b, d)),
        ],
        out_specs=pl.BlockSpec((nbs, c, d), lambda i, j: (i, j, 0)),
        out_shape=jax.ShapeDtypeStruct((bsz, t, d), F32),
        scratch_shapes=[
            pltpu.VMEM((nbs * c, n_cols), F32),
            pltpu.VMEM((nbs, n_heads_a, HA_HEAD_DIM, HA_HEAD_DIM), F32),
            pltpu.VMEM((nbs, n_pairs, LANES, LANES), F32),
            pltpu.VMEM((nbs, n_rkv), F32),
        ],
        compiler_params=pltpu.CompilerParams(
            dimension_semantics=("arbitrary", "arbitrary"), vmem_limit_bytes=vmem),
        name="mixer",
    )(x, x, row(mix_norm), w_in_p, lb_logits.astype(F32), row(hgrn_out_norm), mu_p, row(w0), row(a0),
      wcomb.astype(BF16), row(k_k), row(k_a), row(r_k), row(gn_w), row(gn_b), w_out.astype(BF16))


def kernel(x, ffn1_norm, ffn1_w_gate, ffn1_w_up, ffn1_w_down, mix_norm, w_in, hgrn_lb_logits, hgrn_out_norm, rwkv_shift_mu, rwkv_w0, rwkv_w2, rwkv_a0, rwkv_a2, rwkv_g2, rwkv_k_k, rwkv_k_a, rwkv_r_k, rwkv_gn_w, rwkv_gn_b, w_out, ffn2_norm, ffn2_w_gate, ffn2_w_up, ffn2_w_down, final_norm):
    bsz, t, d = x.shape
    depth = ffn1_norm.shape[0]
    assert depth == 1 and hgrn_lb_logits.shape[0] == depth + 1
    l = 0
    row = lambda vec: vec.reshape(1, -1).astype(F32)
    fg = row(final_norm)
    h = _ffn(x.reshape(bsz * t, d), row(ffn1_norm[l]), ffn1_w_gate[l].astype(BF16),
             ffn1_w_up[l].astype(BF16), ffn1_w_down[l].astype(BF16), fg, final_norm=False)
    h = _mixer(h.reshape(bsz, t, d), mix_norm[l], w_in[l], hgrn_lb_logits, hgrn_out_norm[l],
               rwkv_shift_mu[l], rwkv_w0[l], rwkv_w2[l], rwkv_a0[l], rwkv_a2[l], rwkv_g2[l],
               rwkv_k_k[l], rwkv_k_a[l], rwkv_r_k[l], rwkv_gn_w[l], rwkv_gn_b[l], w_out[l])
    h = _ffn(h.reshape(bsz * t, d), row(ffn2_norm[l]), ffn2_w_gate[l].astype(BF16),
             ffn2_w_up[l].astype(BF16), ffn2_w_down[l].astype(BF16), fg, final_norm=True)
    return h.reshape(bsz, t, d)
```

```python
import functools
import math

import jax
import jax.numpy as jnp
from jax import lax
from jax.experimental import pallas as pl
from jax.experimental.pallas import tpu as pltpu

F32 = jnp.float32
BF16 = jnp.bfloat16

NORM_EPS = 1e-6
RWKV_GN_EPS = 64e-5
L2_EPS = 1e-12
DECAY_SCALE = math.exp(-0.5)

LANES = 128
MXU_COLS = 256
HA_HEAD_DIM = 128
HB_HEAD_DIM = 64
CHUNK = 64
DECAY_LORA, AAA_LORA, GATE_LORA = 32, 32, 96
LORA_PAD = 256
FFN_ROW_TILE = 1024
FFN_COL_TILE = 768
MIXER_GROUP = 4
MIXER_GROUPS_PER_STEP = 2
IN_PROJ_COL_TILE = 512
MIB = 1024 * 1024


def _dot(a, b):
    return jnp.dot(a, b, preferred_element_type=F32)


def _dot_nt(a, b):
    return lax.dot_general(a, b, (((1,), (1,)), ((), ())), preferred_element_type=F32)


def _dot_tn(a, b):
    return lax.dot_general(a, b, (((0,), (0,)), ((), ())), preferred_element_type=F32)


def _split_bf16(x):
    hi = x.astype(BF16)
    lo = (x - hi.astype(F32)).astype(BF16)
    return hi, lo


def _rms_norm(x, g):
    return x * lax.rsqrt(jnp.mean(x * x, axis=-1, keepdims=True) + NORM_EPS) * g


def _sigmoid(x):
    return 0.5 * jnp.tanh(0.5 * x) + 0.5


def _ffn_kernel(x_ref, g_ref, wg_ref, wu_ref, wd_ref, fn_ref, o_ref, *, col_tiles, final_norm):
    x = x_ref[...]
    h = _rms_norm(x, g_ref[...]).astype(BF16)
    acc = jnp.zeros(x.shape, F32)
    for lo, hi in col_tiles:
        gate = _dot(h, wg_ref[:, lo:hi])
        up = _dot(h, wu_ref[:, lo:hi])
        act = (gate * _sigmoid(gate) * up).astype(BF16)
        acc = acc + _dot(act, wd_ref[lo:hi, :])
    y = x + 0.5 * acc
    if final_norm:
        y = _rms_norm(y, fn_ref[...])
    o_ref[...] = y


def _resident(shape):
    return pl.BlockSpec(shape, lambda *_: (0,) * len(shape), pipeline_mode=pl.Buffered(1))


def _ffn(x2d, norm_g, w_gate, w_up, w_down, final_g, *, final_norm):
    n, d = x2d.shape
    d_ff = w_gate.shape[1]
    tm = FFN_ROW_TILE
    assert n % tm == 0 and d_ff % MXU_COLS == 0
    col_tiles = tuple((lo, min(lo + FFN_COL_TILE, d_ff)) for lo in range(0, d_ff, FFN_COL_TILE))
    weight_bytes = 3 * d * d_ff * 2
    tile_bytes = tm * d * 4
    vmem = weight_bytes + 8 * tile_bytes + 3 * tm * FFN_COL_TILE * 4 + 8 * MIB
    return pl.pallas_call(
        functools.partial(_ffn_kernel, col_tiles=col_tiles, final_norm=final_norm),
        grid=(n // tm,),
        in_specs=[
            pl.BlockSpec((tm, d), lambda i: (i, 0)),
            _resident((1, d)),
            _resident((d, d_ff)),
            _resident((d, d_ff)),
            _resident((d_ff, d)),
            _resident((1, d)),
        ],
        out_specs=pl.BlockSpec((tm, d), lambda i: (i, 0)),
        out_shape=jax.ShapeDtypeStruct((n, d), F32),
        compiler_params=pltpu.CompilerParams(
            dimension_semantics=("arbitrary",), vmem_limit_bytes=vmem),
        name="ffn_final" if final_norm else "ffn",
    )(x2d, norm_g, w_gate, w_up, w_down, final_g)


def _iota2(shape, axis):
    return lax.broadcasted_iota(jnp.int32, shape, axis)


def _segsum(x, lane_lo):
    outs = []
    for j in range(x.shape[1] // LANES):
        t = x[:, j * LANES:(j + 1) * LANES]
        lo = jnp.sum(jnp.where(lane_lo, t, 0.0), axis=-1, keepdims=True)
        hi = jnp.sum(jnp.where(lane_lo, 0.0, t), axis=-1, keepdims=True)
        outs.append(jnp.where(lane_lo, lo, hi))
    return jnp.concatenate(outs, axis=1)


def _block_rows(x, idx, fn=None):
    c = CHUNK
    outs = []
    for r in range(0, x.shape[0], c):
        row = x[r + idx:r + idx + 1]
        if fn is not None:
            row = fn(row, x[r + c // 2:r + c // 2 + 1])
        outs.append(jnp.broadcast_to(row, (c, x.shape[1])))
    return jnp.concatenate(outs, axis=0)


def _tile(x, b, j, width):
    return x[b * CHUNK:(b + 1) * CHUNK, j * width:(j + 1) * width]


def _assemble(tiles, nb, nj):
    return jnp.concatenate(
        [jnp.concatenate([tiles[(b, j)] for j in range(nj)], axis=1) for b in range(nb)], axis=0)


def _merge_stages(*stage_lists):
    tagged = []
    for stages in stage_lists:
        tagged += [((i + 0.5) / len(stages), i, fn) for i, fn in enumerate(stages)]
    for _, _, fn in sorted(tagged, key=lambda t: t[:2]):
        fn()


class _Masks:
    def __init__(self, rows):
        c = CHUNK
        ri, ci = _iota2((c, c), 0), _iota2((c, c), 1)
        self.causal = ci <= ri
        bi, bj = _iota2((rows, rows), 0), _iota2((rows, rows), 1)
        same_block = lax.shift_right_logical(bi, 6) == lax.shift_right_logical(bj, 6)
        tri = ((bj <= bi) & same_block).astype(BF16)
        self.tri2 = jnp.concatenate([tri, tri], axis=1)
        gi, gj = _iota2((2 * c, 4 * c), 0), _iota2((2 * c, 4 * c), 1)
        ti, sj = gi & (c - 1), gj & (c - 1)
        self.gmask = (sj < ti) | ((gi >= c) & (sj == ti))
        li, lj = _iota2((LANES, LANES), 0), _iota2((LANES, LANES), 1)
        self.bdmask = lax.shift_right_logical(li, 6) == lax.shift_right_logical(lj, 6)
        self.lane_lo = _iota2((1, LANES), 1) < HB_HEAD_DIM
        self.row0 = (_iota2((rows, 1), 0) & (c - 1)) == 0
        self.lora_lane = _iota2((1, LORA_PAD), 1)


def _hgrn_stages(ctx, hs_ref, b0, nb, w_a, mk):
    hd = HA_HEAD_DIM
    chains = [(b, h) for b in range(nb) for h in range(w_a // hd)]
    st = {}

    def scores():
        qt, kt = ctx['hg_qt'], ctx['hg_kt']
        st['s_old'] = {ch: hs_ref[b0 + ch[0], ch[1]] for ch in chains}
        st['sc'] = {ch: jnp.where(mk.causal, _dot_nt(_tile(qt, *ch, hd), _tile(kt, *ch, hd)), 0.0).astype(BF16)
                    for ch in chains}

    def outputs():
        v, q0, kl, d = ctx['hg_v'], ctx['hg_q0'], ctx['hg_kl'], ctx['hg_d']
        s_old = st['s_old']
        o = {ch: _dot(st['sc'][ch], _tile(v, *ch, hd)) + _dot_nt(_tile(q0, *ch, hd), s_old[ch].astype(BF16))
             for ch in chains}
        for ch in chains:
            hs_ref[b0 + ch[0], ch[1]] = (_tile(d, *ch, hd)[0:1] * s_old[ch]
                                         + _dot_tn(_tile(v, *ch, hd), _tile(kl, *ch, hd)))
        st['o'] = o

    def normalise():
        o = {ch: t * lax.rsqrt(jnp.mean(t * t, axis=-1, keepdims=True) + NORM_EPS) for ch, t in st['o'].items()}
        ctx['o_a'] = _assemble(o, nb, w_a // hd)

    return [scores, outputs, normalise]


def _rwkv_stages(ctx, rs_ref, b0, nb, w_b, mk):
    c = CHUNK
    n_pairs = w_b // LANES
    lane_lo = mk.lane_lo
    pairs = [(b, pr) for b in range(nb) for pr in range(n_pairs)]
    tl = lambda t, pair: _tile(t, pair[0], pair[1], LANES)
    zero = jnp.zeros((c, LANES), BF16)
    lo = lambda t: jnp.where(lane_lo, t, zero)
    hi = lambda t: jnp.where(lane_lo, zero, t)
    cat0 = lambda *ts: jnp.concatenate(ts, axis=0)
    st = {}

    def state_read():
        at, rt, bt, kt, a0, r0, bl, kl, v = ctx['rw']
        st['s_old'] = {pp: rs_ref[b0 + pp[0], pp[1]] for pp in pairs}
        st['z'] = {pp: _dot_nt(cat0(tl(a0, pp), tl(r0, pp)), st['s_old'][pp].astype(BF16)) for pp in pairs}

    def gram():
        at, rt, bt, kt, a0, r0, bl, kl, v = ctx['rw']
        g = {}
        for pp in pairs:
            b_, k_ = tl(bt, pp), tl(kt, pp)
            g[pp] = jnp.where(
                mk.gmask, _dot_nt(cat0(tl(at, pp), tl(rt, pp)), cat0(lo(b_), hi(b_), lo(k_), hi(k_))), 0.0)
        st['g'] = g

    def rhs0():
        v = ctx['rw'][8]
        x, pw = {}, {}
        for pp in pairs:
            pw[pp] = st['g'][pp][:c, :LANES].astype(BF16)
            lak = st['g'][pp][:c, LANES:].astype(BF16)
            x[pp] = st['z'][pp][:c] + _dot(lak, cat0(lo(tl(v, pp)), hi(tl(v, pp))))
        st['x'], st['pw'] = x, pw

    def solve_step():
        x, pw, res = st['x'], st['pw'], {}
        for pp in pairs:
            xb = x[pp].astype(BF16)
            rhs = jnp.concatenate([cat0(lo(xb), hi(xb)), cat0(lo(pw[pp]), hi(pw[pp]))], axis=1)
            res[pp] = _dot(pw[pp], rhs)
        st['x'] = {pp: x[pp] + res[pp][:, :LANES] for pp in pairs}
        st['pw'] = {pp: res[pp][:, LANES:].astype(BF16) for pp in pairs}

    def solve_last():
        x, pw = st['x'], st['pw']
        for pp in pairs:
            xb = x[pp].astype(BF16)
            x[pp] = x[pp] + _dot(pw[pp], cat0(lo(xb), hi(xb)))

    def outputs():
        at, rt, bt, kt, a0, r0, bl, kl, v = ctx['rw']
        dl = ctx['rw_dl']
        y = {}
        for pp in pairs:
            ub, vv = st['x'][pp].astype(BF16), tl(v, pp)
            y[pp] = st['z'][pp][c:] + _dot(st['g'][pp][c:].astype(BF16), cat0(lo(ub), hi(ub), lo(vv), hi(vv)))
            upd = _dot_tn(cat0(ub, vv), cat0(tl(bl, pp), tl(kl, pp)))
            rs_ref[b0 + pp[0], pp[1]] = tl(dl, pp)[0:1] * st['s_old'][pp] + jnp.where(mk.bdmask, upd, 0.0)
        ctx['y'] = _assemble(y, nb, n_pairs)

    n_steps = 0
    n = 1
    while 2 * n < c:
        n_steps += 1
        n *= 2
    return [state_read, gram, rhs0] + [solve_step] * n_steps + [solve_last, outputs]


def _mixer_kernel(x_ref, xn_ref, mn_ref, win_ref, lbl_ref, hon_ref, mu_ref, w0_ref, a0_ref, wcomb_ref,
                  kk_ref, ka_ref, rk_ref, gnw_ref, gnb_ref, wout_ref, o_ref,
                  p_ref, hs_ref, rs_ref, carry_ref, *, nb, n_groups, w_a, w_b):
    c = CHUNK
    rows = nb * c
    d_model = x_ref.shape[-1]
    n_rkv = 3 * w_b + LORA_PAD

    def in_proj_stages(grp, load_x):
        prow = slice(grp * rows, (grp + 1) * rows)
        hn = {}

        def norm():
            x = load_x().reshape(rows, d_model)
            hn['h'] = _rms_norm(x, mn_ref[...]).astype(BF16)

        def cols(lo, hi):
            def project():
                p_ref[prow, lo:hi] = _dot(hn['h'], win_ref[:, lo:hi])
            return project

        n_cols = 4 * w_a + n_rkv
        return [norm] + [cols(lo, min(lo + IN_PROJ_COL_TILE, n_cols))
                         for lo in range(0, n_cols, IN_PROJ_COL_TILE)]

    group_x = lambda grp: (lambda: x_ref[grp * nb:(grp + 1) * nb])

    @pl.when(pl.program_id(1) == 0)
    def _():
        hs_ref[...] = jnp.zeros_like(hs_ref)
        rs_ref[...] = jnp.zeros_like(rs_ref)
        carry_ref[...] = jnp.zeros_like(carry_ref)
        _merge_stages(in_proj_stages(0, group_x(0)))

    mk = _Masks(c)
    logits = lbl_ref[...]
    e = jnp.exp(logits - jnp.max(logits, axis=0, keepdims=True))
    lb = e[0:1] / jnp.sum(e, axis=0, keepdims=True)

    def prep_stages(grp, ctx):
        bctxs = [dict() for _ in range(nb)]
        per_batch = [prep_batch_stages(grp * nb + b, bctxs[b]) for b in range(nb)]

        def collect():
            rows_of = lambda name: jnp.concatenate([bc_[name] for bc_ in bctxs], axis=0)
            for name in ('hg_qt', 'hg_kt', 'hg_kl', 'hg_q0', 'hg_d', 'hg_v', 'bonus', 'gate', 'rw_dl'):
                ctx[name] = rows_of(name)
            ctx['rw'] = tuple(jnp.concatenate([bc_['rw'][i] for bc_ in bctxs], axis=0) for i in range(9))

        return [fn for stage in zip(*per_batch) for fn in stage] + [collect]

    def prep_batch_stages(bidx, ctx):
        prow = slice(bidx * c, (bidx + 1) * c)

        def hgrn_gates():
            q_a = p_ref[prow, 0:w_a]
            f_a = p_ref[prow, w_a:2 * w_a]
            forget = lb + (1.0 - lb) * _sigmoid(f_a)
            ctx['hg_logf'] = _split_bf16(jnp.log(forget))
            ctx['hg_kh'] = 1.0 - forget
            ctx['hg_q'] = q_a * _sigmoid(q_a)

        def hgrn_decays():
            hi, lo = ctx['hg_logf']
            bc = _dot(mk.tri2, jnp.concatenate([hi, lo], axis=0))
            q, kh = ctx['hg_q'], ctx['hg_kh']
            b_ref = _block_rows(bc, c // 2)
            qt = q * jnp.exp(bc - b_ref)
            kt = kh * jnp.exp(b_ref - bc)
            ctx['hg_qt'] = qt.astype(BF16)
            ctx['hg_kt'] = kt.astype(BF16)
            ctx['hg_kl'] = (kt * _block_rows(bc, c - 1, lambda last, ref: jnp.exp(last - ref))).astype(BF16)
            ctx['hg_q0'] = (qt * _block_rows(bc, c - 1, lambda last, ref: jnp.exp(ref))).astype(BF16)
            ctx['hg_d'] = _block_rows(bc, c - 1, lambda last, ref: jnp.exp(last))
            ctx['hg_v'] = p_ref[prow, 2 * w_a:3 * w_a].astype(BF16)

        def rwkv_shift():
            pb = p_ref[prow, 4 * w_a:4 * w_a + n_rkv]
            prev = jnp.broadcast_to(carry_ref[bidx:bidx + 1, :], (c, n_rkv))
            shifted = jnp.where(mk.row0, prev, pltpu.roll(pb, 1, 0))
            carry_ref[bidx:bidx + 1, :] = pb[c - 1:c, :]
            pb = pb + mu_ref[...] * (shifted - pb)
            ctx['r'] = pb[:, 0:w_b]
            ctx['k'] = pb[:, w_b:2 * w_b]
            ctx['v'] = pb[:, 2 * w_b:3 * w_b]
            low = pb[:, 3 * w_b:]
            ll = mk.lora_lane
            act = jnp.where(ll < DECAY_LORA, jnp.tanh(low),
                            jnp.where(ll < DECAY_LORA + AAA_LORA, low,
                                      jnp.where(ll < DECAY_LORA + AAA_LORA + GATE_LORA, _sigmoid(low), 0.0)))
            ctx['act'] = act.astype(BF16)

        def rwkv_lora():
            r, k = ctx['r'], ctx['k']
            ld = _dot(ctx['act'], wcomb_ref[...])
            lw = -DECAY_SCALE * _sigmoid(w0_ref[...] + ld[:, 0:w_b])
            a = _sigmoid(a0_ref[...] + ld[:, w_b:2 * w_b])
            ctx['gate'] = ld[:, 2 * w_b:3 * w_b]
            kk = k * kk_ref[...]
            k = k * (1.0 + (a - 1.0) * ka_ref[...])
            ctx['k'], ctx['a'], ctx['kk'], ctx['lw'] = k, a, kk, lw
            ctx['lw_split'] = _split_bf16(lw)
            ctx['kk_sq'] = kk * kk
            ctx['rk'] = r * k * rk_ref[...]

        def rwkv_sums():
            hi, lo = ctx['lw_split']
            ctx['gc'] = _dot(mk.tri2, jnp.concatenate([hi, lo], axis=0))
            kk = ctx['kk'] * lax.rsqrt(jnp.maximum(_segsum(ctx['kk_sq'], mk.lane_lo), L2_EPS * L2_EPS))
            ctx['bonus'] = _segsum(ctx['rk'], mk.lane_lo) * ctx['v']
            ctx['av'], ctx['bv'] = -kk, kk * ctx['a']

        def rwkv_decays():
            r, k, v, av, bv, lw, gc = (ctx[n] for n in ('r', 'k', 'v', 'av', 'bv', 'lw', 'gc'))
            gref = _block_rows(gc, c // 2)
            e_up = jnp.exp(gc - gref)
            e_dn = jnp.exp(gref - gc)
            at, rt, bt, kt = av * jnp.exp(gc - lw - gref), r * e_up, bv * e_dn, k * e_dn
            s_ref = _block_rows(gc, c - 1, lambda last, ref: jnp.exp(ref))
            s_last = _block_rows(gc, c - 1, lambda last, ref: jnp.exp(last - ref))
            full = (at, rt, bt, kt, at * s_ref, rt * s_ref, bt * s_last, kt * s_last, v)
            ctx['rw'] = tuple(t.astype(BF16) for t in full)
            ctx['rw_dl'] = _block_rows(gc, c - 1, lambda last, ref: jnp.exp(last))

        return [hgrn_gates, hgrn_decays, rwkv_shift, rwkv_lora, rwkv_sums, rwkv_decays]

    def chain_stages(grp, ctx):
        b0 = grp * nb
        return _hgrn_stages(ctx, hs_ref, b0, nb, w_a, mk) + _rwkv_stages(ctx, rs_ref, b0, nb, w_b, mk)

    def post_stages(grp, ctx):
        b0 = grp * nb
        inv_n = 1.0 / HB_HEAD_DIM
        parts = [dict() for _ in range(nb)]

        def batch_stages(b, st):
            brow = slice(b * c, (b + 1) * c)
            prow = slice((b0 + b) * c, (b0 + b + 1) * c)

            def gate_a():
                g_a = p_ref[prow, 3 * w_a:4 * w_a]
                st['o_a'] = (ctx['o_a'][brow] * hon_ref[...] * (g_a * _sigmoid(g_a))).astype(BF16)
                st['y'] = ctx['y'][brow]

            def gn_mean():
                yc = st['y'] - _segsum(st['y'], mk.lane_lo) * inv_n
                st['yc'], st['yc_sq'] = yc, yc * yc

            def gn_var():
                var = _segsum(st['yc_sq'], mk.lane_lo) * inv_n
                yn = st['yc'] * lax.rsqrt(var + RWKV_GN_EPS) * gnw_ref[...] + gnb_ref[...]
                o_b = ((yn + ctx['bonus'][brow]) * ctx['gate'][brow]).astype(BF16)
                st['o'] = jnp.concatenate([st['o_a'], o_b], axis=1)

            return [gate_a, gn_mean, gn_var]

        def out_proj():
            o = jnp.concatenate([st['o'] for st in parts], axis=0)
            x = x_ref[b0:b0 + nb].reshape(rows, d_model)
            o_ref[b0:b0 + nb] = (x + _dot(o, wout_ref[...])).reshape(nb, c, d_model)

        per_batch = [batch_stages(b, parts[b]) for b in range(nb)]
        return [fn for stage in zip(*per_batch) for fn in stage] + [out_proj]

    ctxs = [dict() for _ in range(n_groups)]
    lists = [prep_stages(0, ctxs[0])]
    if n_groups > 1:
        lists.append(in_proj_stages(1, group_x(1)))
    _merge_stages(*lists)
    for grp in range(n_groups):
        lists = [chain_stages(grp, ctxs[grp])]
        if grp + 1 < n_groups:
            lists.append(prep_stages(grp + 1, ctxs[grp + 1]))
        if grp + 2 < n_groups:
            lists.append(in_proj_stages(grp + 2, group_x(grp + 2)))
        if grp >= 1:
            lists.append(post_stages(grp - 1, ctxs[grp - 1]))
        _merge_stages(*lists)
    _merge_stages(post_stages(n_groups - 1, ctxs[n_groups - 1]),
                  in_proj_stages(0, lambda: xn_ref[...]))


def _mixer(x, mix_norm, w_in, lb_logits, hgrn_out_norm, mu, w0, w2, a0, a2, g2, k_k, k_a, r_k,
           gn_w, gn_b, w_out):
    bsz, t, d = x.shape
    w_a = hgrn_out_norm.shape[-1]
    w_b = w0.shape[-1]
    nb, n_groups, c = MIXER_GROUP, MIXER_GROUPS_PER_STEP, CHUNK
    nbs = nb * n_groups
    n_lora = DECAY_LORA + AAA_LORA + GATE_LORA
    assert bsz % nbs == 0 and t % c == 0
    assert w_a % HA_HEAD_DIM == 0 and w_b % LANES == 0
    assert w_in.shape[1] == 4 * w_a + 3 * w_b + n_lora
    pad = LORA_PAD - n_lora
    n_cols = 4 * w_a + 3 * w_b + LORA_PAD
    n_rkv = 3 * w_b + LORA_PAD
    w_in_p = jnp.pad(w_in, ((0, 0), (0, pad))).astype(BF16)
    mu_p = jnp.pad(mu.reshape(1, -1), ((0, 0), (0, pad)))
    wcomb = jnp.zeros((LORA_PAD, 3 * w_b), F32)
    wcomb = wcomb.at[0:DECAY_LORA, 0:w_b].set(w2)
    wcomb = wcomb.at[DECAY_LORA:DECAY_LORA + AAA_LORA, w_b:2 * w_b].set(a2)
    wcomb = wcomb.at[DECAY_LORA + AAA_LORA:n_lora, 2 * w_b:3 * w_b].set(g2)
    row = lambda vec: vec.reshape(1, -1).astype(F32)
    n_pairs = w_b // LANES
    n_heads_a = w_a // HA_HEAD_DIM
    state_bytes = nbs * (n_heads_a * HA_HEAD_DIM * HA_HEAD_DIM + n_pairs * LANES * LANES) * 4
    weight_bytes = (d * n_cols + LORA_PAD * 3 * w_b + (w_a + w_b) * d) * 2
    vmem = weight_bytes + state_bytes + 3 * nbs * c * n_cols * 4 + 4 * nbs * c * d * 4 + 12 * MIB
    return pl.pallas_call(
        functools.partial(_mixer_kernel, nb=nb, n_groups=n_groups, w_a=w_a, w_b=w_b),
        grid=(bsz // nbs, t // c),
        in_specs=[
            pl.BlockSpec((nbs, c, d), lambda i, j: (i, j, 0)),
            pl.BlockSpec((nb, c, d), lambda i, j: (i * n_groups, jnp.minimum(j + 1, t // c - 1), 0)),
            _resident((1, d)),
            _resident((d, n_cols)),
            _resident(lb_logits.shape),
            _resident((1, w_a)),
            _resident((1, n_rkv)),
            _resident((1, w_b)),
            _resident((1, w_b)),
            _resident((LORA_PAD, 3 * w_b)),
            _resident((1, w_b)),
            _resident((1, w_b)),
            _resident((1, w_b)),
            _resident((1, w_b)),
            _resident((1, w_b)),
            _resident((w_a + w_b, d)),
        ],
        out_specs=pl.BlockSpec((nbs, c, d), lambda i, j: (i, j, 0)),
        out_shape=jax.ShapeDtypeStruct((bsz, t, d), F32),
        scratch_shapes=[
            pltpu.VMEM((nbs * c, n_cols), F32),
            pltpu.VMEM((nbs, n_heads_a, HA_HEAD_DIM, HA_HEAD_DIM), F32),
            pltpu.VMEM((nbs, n_pairs, LANES, LANES), F32),
            pltpu.VMEM((nbs, n_rkv), F32),
        ],
        compiler_params=pltpu.CompilerParams(
            dimension_semantics=("arbitrary", "arbitrary"), vmem_limit_bytes=vmem),
        name="mixer",
    )(x, x, row(mix_norm), w_in_p, lb_logits.astype(F32), row(hgrn_out_norm), mu_p, row(w0), row(a0),
      wcomb.astype(BF16), row(k_k), row(k_a), row(r_k), row(gn_w), row(gn_b), w_out.astype(BF16))


def kernel(x, ffn1_norm, ffn1_w_gate, ffn1_w_up, ffn1_w_down, mix_norm, w_in, hgrn_lb_logits, hgrn_out_norm, rwkv_shift_mu, rwkv_w0, rwkv_w2, rwkv_a0, rwkv_a2, rwkv_g2, rwkv_k_k, rwkv_k_a, rwkv_r_k, rwkv_gn_w, rwkv_gn_b, w_out, ffn2_norm, ffn2_w_gate, ffn2_w_up, ffn2_w_down, final_norm):
    bsz, t, d = x.shape
    depth = ffn1_norm.shape[0]
    assert depth == 1 and hgrn_lb_logits.shape[0] == depth + 1
    l = 0
    row = lambda vec: vec.reshape(1, -1).astype(F32)
    fg = row(final_norm)
    h = _ffn(x.reshape(bsz * t, d), row(ffn1_norm[l]), ffn1_w_gate[l].astype(BF16),
             ffn1_w_up[l].astype(BF16), ffn1_w_down[l].astype(BF16), fg, final_norm=False)
    h = _mixer(h.reshape(bsz, t, d), mix_norm[l], w_in[l], hgrn_lb_logits, hgrn_out_norm[l],
               rwkv_shift_mu[l], rwkv_w0[l], rwkv_w2[l], rwkv_a0[l], rwkv_a2[l], rwkv_g2[l],
               rwkv_k_k[l], rwkv_k_a[l], rwkv_r_k[l], rwkv_gn_w[l], rwkv_gn_b[l], w_out[l])
    h = _ffn(h.reshape(bsz * t, d), row(ffn2_norm[l]), ffn2_w_gate[l].astype(BF16),
             ffn2_w_up[l].astype(BF16), ffn2_w_down[l].astype(BF16), fg, final_norm=True)
    return h.reshape(bsz, t, d)
```

```python
import functools
import math

import jax
import jax.numpy as jnp
from jax import lax
from jax.experimental import pallas as pl
from jax.experimental.pallas import tpu as pltpu

F32 = jnp.float32
BF16 = jnp.bfloat16

NORM_EPS = 1e-6
RWKV_GN_EPS = 64e-5
L2_EPS = 1e-12
DECAY_SCALE = math.exp(-0.5)

LANES = 128
MXU_COLS = 256
HA_HEAD_DIM = 128
HB_HEAD_DIM = 64
CHUNK = 64
DECAY_LORA, AAA_LORA, GATE_LORA = 32, 32, 96
LORA_PAD = 256
FFN_ROW_TILE = 1024
FFN_COL_TILE = 768
MIXER_GROUP = 4
MIXER_GROUPS_PER_STEP = 2
IN_PROJ_COL_TILE = 512
MIB = 1024 * 1024


def _dot(a, b):
    return jnp.dot(a, b, preferred_element_type=F32)


def _dot_nt(a, b):
    return lax.dot_general(a, b, (((1,), (1,)), ((), ())), preferred_element_type=F32)


def _dot_tn(a, b):
    return lax.dot_general(a, b, (((0,), (0,)), ((), ())), preferred_element_type=F32)


def _split_bf16(x):
    hi = x.astype(BF16)
    lo = (x - hi.astype(F32)).astype(BF16)
    return hi, lo


def _rms_norm(x, g):
    return x * lax.rsqrt(jnp.mean(x * x, axis=-1, keepdims=True) + NORM_EPS) * g


def _sigmoid(x):
    return 0.5 * jnp.tanh(0.5 * x) + 0.5


def _ffn_kernel(x_ref, g_ref, wg_ref, wu_ref, wd_ref, fn_ref, o_ref, *, col_tiles, final_norm):
    x = x_ref[...]
    h = _rms_norm(x, g_ref[...]).astype(BF16)
    acc = jnp.zeros(x.shape, F32)
    for lo, hi in col_tiles:
        gate = _dot(h, wg_ref[:, lo:hi])
        up = _dot(h, wu_ref[:, lo:hi])
        act = (gate * _sigmoid(gate) * up).astype(BF16)
        acc = acc + _dot(act, wd_ref[lo:hi, :])
    y = x + 0.5 * acc
    if final_norm:
        y = _rms_norm(y, fn_ref[...])
    o_ref[...] = y


def _resident(shape):
    return pl.BlockSpec(shape, lambda *_: (0,) * len(shape), pipeline_mode=pl.Buffered(1))


def _ffn(x2d, norm_g, w_gate, w_up, w_down, final_g, *, final_norm):
    n, d = x2d.shape
    d_ff = w_gate.shape[1]
    tm = FFN_ROW_TILE
    assert n % tm == 0 and d_ff % MXU_COLS == 0
    col_tiles = tuple((lo, min(lo + FFN_COL_TILE, d_ff)) for lo in range(0, d_ff, FFN_COL_TILE))
    weight_bytes = 3 * d * d_ff * 2
    tile_bytes = tm * d * 4
    vmem = weight_bytes + 8 * tile_bytes + 3 * tm * FFN_COL_TILE * 4 + 8 * MIB
    return pl.pallas_call(
        functools.partial(_ffn_kernel, col_tiles=col_tiles, final_norm=final_norm),
        grid=(n // tm,),
        in_specs=[
            pl.BlockSpec((tm, d), lambda i: (i, 0)),
            _resident((1, d)),
            _resident((d, d_ff)),
            _resident((d, d_ff)),
            _resident((d_ff, d)),
            _resident((1, d)),
        ],
        out_specs=pl.BlockSpec((tm, d), lambda i: (i, 0)),
        out_shape=jax.ShapeDtypeStruct((n, d), F32),
        compiler_params=pltpu.CompilerParams(
            dimension_semantics=("arbitrary",), vmem_limit_bytes=vmem),
        name="ffn_final" if final_norm else "ffn",
    )(x2d, norm_g, w_gate, w_up, w_down, final_g)


def _iota2(shape, axis):
    return lax.broadcasted_iota(jnp.int32, shape, axis)


def _segsum(x, lane_lo):
    outs = []
    for j in range(x.shape[1] // LANES):
        t = x[:, j * LANES:(j + 1) * LANES]
        lo = jnp.sum(jnp.where(lane_lo, t, 0.0), axis=-1, keepdims=True)
        hi = jnp.sum(jnp.where(lane_lo, 0.0, t), axis=-1, keepdims=True)
        outs.append(jnp.where(lane_lo, lo, hi))
    return jnp.concatenate(outs, axis=1)


def _block_rows(x, idx, fn=None):
    c = CHUNK
    outs = []
    for r in range(0, x.shape[0], c):
        row = x[r + idx:r + idx + 1]
        if fn is not None:
            row = fn(row, x[r + c // 2:r + c // 2 + 1])
        outs.append(jnp.broadcast_to(row, (c, x.shape[1])))
    return jnp.concatenate(outs, axis=0)


def _tile(x, b, j, width):
    return x[b * CHUNK:(b + 1) * CHUNK, j * width:(j + 1) * width]


def _assemble(tiles, nb, nj):
    return jnp.concatenate(
        [jnp.concatenate([tiles[(b, j)] for j in range(nj)], axis=1) for b in range(nb)], axis=0)


def _merge_stages(*stage_lists):
    tagged = []
    for stages in stage_lists:
        tagged += [((i + 0.5) / len(stages), i, fn) for i, fn in enumerate(stages)]
    for _, _, fn in sorted(tagged, key=lambda t: t[:2]):
        fn()


class _Masks:
    def __init__(self, rows):
        c = CHUNK
        ri, ci = _iota2((c, c), 0), _iota2((c, c), 1)
        self.causal = ci <= ri
        bi, bj = _iota2((rows, rows), 0), _iota2((rows, rows), 1)
        same_block = lax.shift_right_logical(bi, 6) == lax.shift_right_logical(bj, 6)
        tri = ((bj <= bi) & same_block).astype(BF16)
        self.tri2 = jnp.concatenate([tri, tri], axis=1)
        gi, gj = _iota2((2 * c, 4 * c), 0), _iota2((2 * c, 4 * c), 1)
        ti, sj = gi & (c - 1), gj & (c - 1)
        self.gmask = (sj < ti) | ((gi >= c) & (sj == ti))
        li, lj = _iota2((LANES, LANES), 0), _iota2((LANES, LANES), 1)
        self.bdmask = lax.shift_right_logical(li, 6) == lax.shift_right_logical(lj, 6)
        self.lane_lo = _iota2((1, LANES), 1) < HB_HEAD_DIM
        self.row0 = (_iota2((rows, 1), 0) & (c - 1)) == 0
        self.lora_lane = _iota2((1, LORA_PAD), 1)


def _hgrn_stages(ctx, hs_ref, b0, nb, w_a, mk):
    hd = HA_HEAD_DIM
    chains = [(b, h) for b in range(nb) for h in range(w_a // hd)]
    st = {}

    def scores():
        qt, kt = ctx['hg_qt'], ctx['hg_kt']
        st['s_old'] = {ch: hs_ref[b0 + ch[0], ch[1]] for ch in chains}
        st['sc'] = {ch: jnp.where(mk.causal, _dot_nt(_tile(qt, *ch, hd), _tile(kt, *ch, hd)), 0.0).astype(BF16)
                    for ch in chains}

    def outputs():
        v, q0, kl, d = ctx['hg_v'], ctx['hg_q0'], ctx['hg_kl'], ctx['hg_d']
        s_old = st['s_old']
        o = {ch: _dot(st['sc'][ch], _tile(v, *ch, hd)) + _dot_nt(_tile(q0, *ch, hd), s_old[ch].astype(BF16))
             for ch in chains}
        for ch in chains:
            hs_ref[b0 + ch[0], ch[1]] = (_tile(d, *ch, hd)[0:1] * s_old[ch]
                                         + _dot_tn(_tile(v, *ch, hd), _tile(kl, *ch, hd)))
        st['o'] = o

    def normalise():
        o = {ch: t * lax.rsqrt(jnp.mean(t * t, axis=-1, keepdims=True) + NORM_EPS) for ch, t in st['o'].items()}
        ctx['o_a'] = _assemble(o, nb, w_a // hd)

    return [scores, outputs, normalise]


def _rwkv_stages(ctx, rs_ref, b0, nb, w_b, mk):
    c = CHUNK
    n_pairs = w_b // LANES
    lane_lo = mk.lane_lo
    pairs = [(b, pr) for b in range(nb) for pr in range(n_pairs)]
    tl = lambda t, pair: _tile(t, pair[0], pair[1], LANES)
    zero = jnp.zeros((c, LANES), BF16)
    lo = lambda t: jnp.where(lane_lo, t, zero)
    hi = lambda t: jnp.where(lane_lo, zero, t)
    cat0 = lambda *ts: jnp.concatenate(ts, axis=0)
    st = {}

    def state_read():
        at, rt, bt, kt, a0, r0, bl, kl, v = ctx['rw']
        st['s_old'] = {pp: rs_ref[b0 + pp[0], pp[1]] for pp in pairs}
        st['z'] = {pp: _dot_nt(cat0(tl(a0, pp), tl(r0, pp)), st['s_old'][pp].astype(BF16)) for pp in pairs}

    def gram():
        at, rt, bt, kt, a0, r0, bl, kl, v = ctx['rw']
        g = {}
        for pp in pairs:
            b_, k_ = tl(bt, pp), tl(kt, pp)
            g[pp] = jnp.where(
                mk.gmask, _dot_nt(cat0(tl(at, pp), tl(rt, pp)), cat0(lo(b_), hi(b_), lo(k_), hi(k_))), 0.0)
        st['g'] = g

    def rhs0():
        v = ctx['rw'][8]
        x, pw = {}, {}
        for pp in pairs:
            pw[pp] = st['g'][pp][:c, :LANES].astype(BF16)
            lak = st['g'][pp][:c, LANES:].astype(BF16)
            x[pp] = st['z'][pp][:c] + _dot(lak, cat0(lo(tl(v, pp)), hi(tl(v, pp))))
        st['x'], st['pw'] = x, pw

    def solve_step():
        x, pw, res = st['x'], st['pw'], {}
        for pp in pairs:
            xb = x[pp].astype(BF16)
            rhs = jnp.concatenate([cat0(lo(xb), hi(xb)), cat0(lo(pw[pp]), hi(pw[pp]))], axis=1)
            res[pp] = _dot(pw[pp], rhs)
        st['x'] = {pp: x[pp] + res[pp][:, :LANES] for pp in pairs}
        st['pw'] = {pp: res[pp][:, LANES:].astype(BF16) for pp in pairs}

    def solve_last():
        x, pw = st['x'], st['pw']
        for pp in pairs:
            xb = x[pp].astype(BF16)
            x[pp] = x[pp] + _dot(pw[pp], cat0(lo(xb), hi(xb)))

    def outputs():
        at, rt, bt, kt, a0, r0, bl, kl, v = ctx['rw']
        dl = ctx['rw_dl']
        y = {}
        for pp in pairs:
            ub, vv = st['x'][pp].astype(BF16), tl(v, pp)
            y[pp] = st['z'][pp][c:] + _dot(st['g'][pp][c:].astype(BF16), cat0(lo(ub), hi(ub), lo(vv), hi(vv)))
            upd = _dot_tn(cat0(ub, vv), cat0(tl(bl, pp), tl(kl, pp)))
            rs_ref[b0 + pp[0], pp[1]] = tl(dl, pp)[0:1] * st['s_old'][pp] + jnp.where(mk.bdmask, upd, 0.0)
        ctx['y'] = _assemble(y, nb, n_pairs)

    n_steps = 0
    n = 1
    while 2 * n < c:
        n_steps += 1
        n *= 2
    return [state_read, gram, rhs0] + [solve_step] * n_steps + [solve_last, outputs]


def _mixer_kernel(x_ref, xn_ref, mn_ref, win_ref, lbl_ref, hon_ref, mu_ref, w0_ref, a0_ref, wcomb_ref,
                  kk_ref, ka_ref, rk_ref, gnw_ref, gnb_ref, wout_ref, o_ref,
                  p_ref, hs_ref, rs_ref, carry_ref, *, nb, n_groups, w_a, w_b):
    c = CHUNK
    rows = nb * c
    d_model = x_ref.shape[-1]
    n_rkv = 3 * w_b + LORA_PAD

    def in_proj_stages(grp, load_x):
        prow = slice(grp * rows, (grp + 1) * rows)
        hn = {}

        def norm():
            x = load_x().reshape(rows, d_model)
            hn['h'] = _rms_norm(x, mn_ref[...]).astype(BF16)

        def cols(lo, hi):
            def project():
                p_ref[prow, lo:hi] = _dot(hn['h'], win_ref[:, lo:hi])
            return project

        n_cols = 4 * w_a + n_rkv
        return [norm] + [cols(lo, min(lo + IN_PROJ_COL_TILE, n_cols))
                         for lo in range(0, n_cols, IN_PROJ_COL_TILE)]

    group_x = lambda grp: (lambda: x_ref[grp * nb:(grp + 1) * nb])

    @pl.when(pl.program_id(1) == 0)
    def _():
        hs_ref[...] = jnp.zeros_like(hs_ref)
        rs_ref[...] = jnp.zeros_like(rs_ref)
        carry_ref[...] = jnp.zeros_like(carry_ref)
        _merge_stages(in_proj_stages(0, group_x(0)))

    mk = _Masks(c)
    logits = lbl_ref[...]
    e = jnp.exp(logits - jnp.max(logits, axis=0, keepdims=True))
    lb = e[0:1] / jnp.sum(e, axis=0, keepdims=True)

    def prep_stages(grp, ctx):
        bctxs = [dict() for _ in range(nb)]
        per_batch = [prep_batch_stages(grp * nb + b, bctxs[b]) for b in range(nb)]

        def collect():
            rows_of = lambda name: jnp.concatenate([bc_[name] for bc_ in bctxs], axis=0)
            for name in ('hg_qt', 'hg_kt', 'hg_kl', 'hg_q0', 'hg_d', 'hg_v', 'hg_gate', 'bonus', 'gate', 'rw_dl'):
                ctx[name] = rows_of(name)
            ctx['rw'] = tuple(jnp.concatenate([bc_['rw'][i] for bc_ in bctxs], axis=0) for i in range(9))

        return [fn for stage in zip(*per_batch) for fn in stage] + [collect]

    def prep_batch_stages(bidx, ctx):
        prow = slice(bidx * c, (bidx + 1) * c)

        def hgrn_gates():
            q_a = p_ref[prow, 0:w_a]
            f_a = p_ref[prow, w_a:2 * w_a]
            forget = lb + (1.0 - lb) * _sigmoid(f_a)
            ctx['hg_logf'] = _split_bf16(jnp.log(forget))
            ctx['hg_kh'] = 1.0 - forget
            ctx['hg_q'] = q_a * _sigmoid(q_a)
            g_a = p_ref[prow, 3 * w_a:4 * w_a]
            ctx['hg_gate'] = hon_ref[...] * (g_a * _sigmoid(g_a))

        def hgrn_decays():
            hi, lo = ctx['hg_logf']
            bc = _dot(mk.tri2, jnp.concatenate([hi, lo], axis=0))
            q, kh = ctx['hg_q'], ctx['hg_kh']
            b_ref = _block_rows(bc, c // 2)
            qt = q * jnp.exp(bc - b_ref)
            kt = kh * jnp.exp(b_ref - bc)
            ctx['hg_qt'] = qt.astype(BF16)
            ctx['hg_kt'] = kt.astype(BF16)
            ctx['hg_kl'] = (kt * _block_rows(bc, c - 1, lambda last, ref: jnp.exp(last - ref))).astype(BF16)
            ctx['hg_q0'] = (qt * _block_rows(bc, c - 1, lambda last, ref: jnp.exp(ref))).astype(BF16)
            ctx['hg_d'] = _block_rows(bc, c - 1, lambda last, ref: jnp.exp(last))
            ctx['hg_v'] = p_ref[prow, 2 * w_a:3 * w_a].astype(BF16)

        def rwkv_shift():
            pb = p_ref[prow, 4 * w_a:4 * w_a + n_rkv]
            prev = jnp.broadcast_to(carry_ref[bidx:bidx + 1, :], (c, n_rkv))
            shifted = jnp.where(mk.row0, prev, pltpu.roll(pb, 1, 0))
            carry_ref[bidx:bidx + 1, :] = pb[c - 1:c, :]
            pb = pb + mu_ref[...] * (shifted - pb)
            ctx['r'] = pb[:, 0:w_b]
            ctx['k'] = pb[:, w_b:2 * w_b]
            ctx['v'] = pb[:, 2 * w_b:3 * w_b]
            low = pb[:, 3 * w_b:]
            ll = mk.lora_lane
            act = jnp.where(ll < DECAY_LORA, jnp.tanh(low),
                            jnp.where(ll < DECAY_LORA + AAA_LORA, low,
                                      jnp.where(ll < DECAY_LORA + AAA_LORA + GATE_LORA, _sigmoid(low), 0.0)))
            ctx['act'] = act.astype(BF16)

        def rwkv_lora():
            r, k = ctx['r'], ctx['k']
            ld = _dot(ctx['act'], wcomb_ref[...])
            lw = -DECAY_SCALE * _sigmoid(w0_ref[...] + ld[:, 0:w_b])
            a = _sigmoid(a0_ref[...] + ld[:, w_b:2 * w_b])
            ctx['gate'] = ld[:, 2 * w_b:3 * w_b]
            kk = k * kk_ref[...]
            k = k * (1.0 + (a - 1.0) * ka_ref[...])
            ctx['k'], ctx['a'], ctx['kk'], ctx['lw'] = k, a, kk, lw
            ctx['lw_split'] = _split_bf16(lw)
            ctx['kk_sq'] = kk * kk
            ctx['rk'] = r * k * rk_ref[...]

        def rwkv_sums():
            hi, lo = ctx['lw_split']
            ctx['gc'] = _dot(mk.tri2, jnp.concatenate([hi, lo], axis=0))
            kk = ctx['kk'] * lax.rsqrt(jnp.maximum(_segsum(ctx['kk_sq'], mk.lane_lo), L2_EPS * L2_EPS))
            ctx['bonus'] = _segsum(ctx['rk'], mk.lane_lo) * ctx['v']
            ctx['av'], ctx['bv'] = -kk, kk * ctx['a']

        def rwkv_decays():
            r, k, v, av, bv, lw, gc = (ctx[n] for n in ('r', 'k', 'v', 'av', 'bv', 'lw', 'gc'))
            gref = _block_rows(gc, c // 2)
            e_up = jnp.exp(gc - gref)
            e_dn = jnp.exp(gref - gc)
            at, rt, bt, kt = av * jnp.exp(gc - lw - gref), r * e_up, bv * e_dn, k * e_dn
            s_ref = _block_rows(gc, c - 1, lambda last, ref: jnp.exp(ref))
            s_last = _block_rows(gc, c - 1, lambda last, ref: jnp.exp(last - ref))
            full = (at, rt, bt, kt, at * s_ref, rt * s_ref, bt * s_last, kt * s_last, v)
            ctx['rw'] = tuple(t.astype(BF16) for t in full)
            ctx['rw_dl'] = _block_rows(gc, c - 1, lambda last, ref: jnp.exp(last))

        return [hgrn_gates, hgrn_decays, rwkv_shift, rwkv_lora, rwkv_sums, rwkv_decays]

    def chain_stages(grp, ctx):
        b0 = grp * nb
        return _hgrn_stages(ctx, hs_ref, b0, nb, w_a, mk) + _rwkv_stages(ctx, rs_ref, b0, nb, w_b, mk)

    def post_stages(grp, ctx):
        b0 = grp * nb
        inv_n = 1.0 / HB_HEAD_DIM
        parts = [dict() for _ in range(nb)]

        def batch_stages(b, st):
            brow = slice(b * c, (b + 1) * c)
            prow = slice((b0 + b) * c, (b0 + b + 1) * c)

            def gate_a():
                st['o_a'] = (ctx['o_a'][brow] * ctx['hg_gate'][brow]).astype(BF16)
                st['y'] = ctx['y'][brow]

            def gn_mean():
                yc = st['y'] - _segsum(st['y'], mk.lane_lo) * inv_n
                st['yc'], st['yc_sq'] = yc, yc * yc

            def gn_var():
                var = _segsum(st['yc_sq'], mk.lane_lo) * inv_n
                yn = st['yc'] * lax.rsqrt(var + RWKV_GN_EPS) * gnw_ref[...] + gnb_ref[...]
                o_b = ((yn + ctx['bonus'][brow]) * ctx['gate'][brow]).astype(BF16)
                st['o'] = jnp.concatenate([st['o_a'], o_b], axis=1)

            return [gate_a, gn_mean, gn_var]

        def out_proj():
            o = jnp.concatenate([st['o'] for st in parts], axis=0)
            x = x_ref[b0:b0 + nb].reshape(rows, d_model)
            o_ref[b0:b0 + nb] = (x + _dot(o, wout_ref[...])).reshape(nb, c, d_model)

        per_batch = [batch_stages(b, parts[b]) for b in range(nb)]
        return [fn for stage in zip(*per_batch) for fn in stage] + [out_proj]

    ctxs = [dict() for _ in range(n_groups)]
    lists = [prep_stages(0, ctxs[0])]
    if n_groups > 1:
        lists.append(in_proj_stages(1, group_x(1)))
    _merge_stages(*lists)
    next_proj = in_proj_stages(0, lambda: xn_ref[...])
    n_early = 1 + (len(next_proj) - 1) // 2
    for grp in range(n_groups):
        lists = [chain_stages(grp, ctxs[grp])]
        if grp + 1 < n_groups:
            lists.append(prep_stages(grp + 1, ctxs[grp + 1]))
        if grp + 2 < n_groups:
            lists.append(in_proj_stages(grp + 2, group_x(grp + 2)))
        if grp >= 1:
            lists.append(post_stages(grp - 1, ctxs[grp - 1]))
        if grp == 0:
            lists.append(next_proj[:n_early])
        _merge_stages(*lists)
    _merge_stages(post_stages(n_groups - 1, ctxs[n_groups - 1]), next_proj[n_early:])


def _mixer(x, mix_norm, w_in, lb_logits, hgrn_out_norm, mu, w0, w2, a0, a2, g2, k_k, k_a, r_k,
           gn_w, gn_b, w_out):
    bsz, t, d = x.shape
    w_a = hgrn_out_norm.shape[-1]
    w_b = w0.shape[-1]
    nb, n_groups, c = MIXER_GROUP, MIXER_GROUPS_PER_STEP, CHUNK
    nbs = nb * n_groups
    n_lora = DECAY_LORA + AAA_LORA + GATE_LORA
    assert bsz % nbs == 0 and t % c == 0
    assert w_a % HA_HEAD_DIM == 0 and w_b % LANES == 0
    assert w_in.shape[1] == 4 * w_a + 3 * w_b + n_lora
    pad = LORA_PAD - n_lora
    n_cols = 4 * w_a + 3 * w_b + LORA_PAD
    n_rkv = 3 * w_b + LORA_PAD
    w_in_p = jnp.pad(w_in, ((0, 0), (0, pad))).astype(BF16)
    mu_p = jnp.pad(mu.reshape(1, -1), ((0, 0), (0, pad)))
    wcomb = jnp.zeros((LORA_PAD, 3 * w_b), F32)
    wcomb = wcomb.at[0:DECAY_LORA, 0:w_b].set(w2)
    wcomb = wcomb.at[DECAY_LORA:DECAY_LORA + AAA_LORA, w_b:2 * w_b].set(a2)
    wcomb = wcomb.at[DECAY_LORA + AAA_LORA:n_lora, 2 * w_b:3 * w_b].set(g2)
    row = lambda vec: vec.reshape(1, -1).astype(F32)
    n_pairs = w_b // LANES
    n_heads_a = w_a // HA_HEAD_DIM
    state_bytes = nbs * (n_heads_a * HA_HEAD_DIM * HA_HEAD_DIM + n_pairs * LANES * LANES) * 4
    weight_bytes = (d * n_cols + LORA_PAD * 3 * w_b + (w_a + w_b) * d) * 2
    vmem = weight_bytes + state_bytes + 3 * nbs * c * n_cols * 4 + 4 * nbs * c * d * 4 + 12 * MIB
    return pl.pallas_call(
        functools.partial(_mixer_kernel, nb=nb, n_groups=n_groups, w_a=w_a, w_b=w_b),
        grid=(bsz // nbs, t // c),
        in_specs=[
            pl.BlockSpec((nbs, c, d), lambda i, j: (i, j, 0)),
            pl.BlockSpec((nb, c, d), lambda i, j: (i * n_groups, jnp.minimum(j + 1, t // c - 1), 0)),
            _resident((1, d)),
            _resident((d, n_cols)),
            _resident(lb_logits.shape),
            _resident((1, w_a)),
            _resident((1, n_rkv)),
            _resident((1, w_b)),
            _resident((1, w_b)),
            _resident((LORA_PAD, 3 * w_b)),
            _resident((1, w_b)),
            _resident((1, w_b)),
            _resident((1, w_b)),
            _resident((1, w_b)),
            _resident((1, w_b)),
            _resident((w_a + w_b, d)),
        ],
        out_specs=pl.BlockSpec((nbs, c, d), lambda i, j: (i, j, 0)),
        out_shape=jax.ShapeDtypeStruct((bsz, t, d), F32),
        scratch_shapes=[
            pltpu.VMEM((nbs * c, n_cols), F32),
            pltpu.VMEM((nbs, n_heads_a, HA_HEAD_DIM, HA_HEAD_DIM), F32),
            pltpu.VMEM((nbs, n_pairs, LANES, LANES), F32),
            pltpu.VMEM((nbs, n_rkv), F32),
        ],
        compiler_params=pltpu.CompilerParams(
            dimension_semantics=("arbitrary", "arbitrary"), vmem_limit_bytes=vmem),
        name="mixer",
    )(x, x, row(mix_norm), w_in_p, lb_logits.astype(F32), row(hgrn_out_norm), mu_p, row(w0), row(a0),
      wcomb.astype(BF16), row(k_k), row(k_a), row(r_k), row(gn_w), row(gn_b), w_out.astype(BF16))


def kernel(x, ffn1_norm, ffn1_w_gate, ffn1_w_up, ffn1_w_down, mix_norm, w_in, hgrn_lb_logits, hgrn_out_norm, rwkv_shift_mu, rwkv_w0, rwkv_w2, rwkv_a0, rwkv_a2, rwkv_g2, rwkv_k_k, rwkv_k_a, rwkv_r_k, rwkv_gn_w, rwkv_gn_b, w_out, ffn2_norm, ffn2_w_gate, ffn2_w_up, ffn2_w_down, final_norm):
    bsz, t, d = x.shape
    depth = ffn1_norm.shape[0]
    assert depth == 1 and hgrn_lb_logits.shape[0] == depth + 1
    l = 0
    row = lambda vec: vec.reshape(1, -1).astype(F32)
    fg = row(final_norm)
    h = _ffn(x.reshape(bsz * t, d), row(ffn1_norm[l]), ffn1_w_gate[l].astype(BF16),
             ffn1_w_up[l].astype(BF16), ffn1_w_down[l].astype(BF16), fg, final_norm=False)
    h = _mixer(h.reshape(bsz, t, d), mix_norm[l], w_in[l], hgrn_lb_logits, hgrn_out_norm[l],
               rwkv_shift_mu[l], rwkv_w0[l], rwkv_w2[l], rwkv_a0[l], rwkv_a2[l], rwkv_g2[l],
               rwkv_k_k[l], rwkv_k_a[l], rwkv_r_k[l], rwkv_gn_w[l], rwkv_gn_b[l], w_out[l])
    h = _ffn(h.reshape(bsz * t, d), row(ffn2_norm[l]), ffn2_w_gate[l].astype(BF16),
             ffn2_w_up[l].astype(BF16), ffn2_w_down[l].astype(BF16), fg, final_norm=True)
    return h.reshape(bsz, t, d)
```

```python
import functools
import math

import jax
import jax.numpy as jnp
from jax import lax
from jax.experimental import pallas as pl
from jax.experimental.pallas import tpu as pltpu

F32 = jnp.float32
BF16 = jnp.bfloat16

NORM_EPS = 1e-6
RWKV_GN_EPS = 64e-5
L2_EPS = 1e-12
DECAY_SCALE = math.exp(-0.5)

LANES = 128
MXU_COLS = 256
HA_HEAD_DIM = 128
HB_HEAD_DIM = 64
CHUNK = 64
DECAY_LORA, AAA_LORA, GATE_LORA = 32, 32, 96
LORA_PAD = 256
FFN_ROW_TILE = 1024
FFN_COL_TILE = 768
MIXER_GROUP = 4
IN_PROJ_COL_TILE = 512
MIB = 1024 * 1024


def _dot(a, b):
    return jnp.dot(a, b, preferred_element_type=F32)


def _dot_nt(a, b):
    return lax.dot_general(a, b, (((1,), (1,)), ((), ())), preferred_element_type=F32)


def _dot_tn(a, b):
    return lax.dot_general(a, b, (((0,), (0,)), ((), ())), preferred_element_type=F32)


def _split_bf16(x):
    hi = x.astype(BF16)
    lo = (x - hi.astype(F32)).astype(BF16)
    return hi, lo


def _rms_norm(x, g):
    return x * lax.rsqrt(jnp.mean(x * x, axis=-1, keepdims=True) + NORM_EPS) * g


def _sigmoid(x):
    return 0.5 * jnp.tanh(0.5 * x) + 0.5


def _ffn_kernel(x_ref, g_ref, wg_ref, wu_ref, wd_ref, fn_ref, o_ref, *, col_tiles, final_norm):
    x = x_ref[...]
    h = _rms_norm(x, g_ref[...]).astype(BF16)
    acc = jnp.zeros(x.shape, F32)
    for lo, hi in col_tiles:
        gate = _dot(h, wg_ref[:, lo:hi])
        up = _dot(h, wu_ref[:, lo:hi])
        act = (gate * _sigmoid(gate) * up).astype(BF16)
        acc = acc + _dot(act, wd_ref[lo:hi, :])
    y = x + 0.5 * acc
    if final_norm:
        y = _rms_norm(y, fn_ref[...])
    o_ref[...] = y


def _resident(shape):
    return pl.BlockSpec(shape, lambda *_: (0,) * len(shape), pipeline_mode=pl.Buffered(1))


def _ffn(x2d, norm_g, w_gate, w_up, w_down, final_g, *, final_norm):
    n, d = x2d.shape
    d_ff = w_gate.shape[1]
    tm = FFN_ROW_TILE
    assert n % tm == 0 and d_ff % MXU_COLS == 0
    col_tiles = tuple((lo, min(lo + FFN_COL_TILE, d_ff)) for lo in range(0, d_ff, FFN_COL_TILE))
    weight_bytes = 3 * d * d_ff * 2
    tile_bytes = tm * d * 4
    vmem = weight_bytes + 8 * tile_bytes + 3 * tm * FFN_COL_TILE * 4 + 8 * MIB
    return pl.pallas_call(
        functools.partial(_ffn_kernel, col_tiles=col_tiles, final_norm=final_norm),
        grid=(n // tm,),
        in_specs=[
            pl.BlockSpec((tm, d), lambda i: (i, 0)),
            _resident((1, d)),
            _resident((d, d_ff)),
            _resident((d, d_ff)),
            _resident((d_ff, d)),
            _resident((1, d)),
        ],
        out_specs=pl.BlockSpec((tm, d), lambda i: (i, 0)),
        out_shape=jax.ShapeDtypeStruct((n, d), F32),
        compiler_params=pltpu.CompilerParams(
            dimension_semantics=("arbitrary",), vmem_limit_bytes=vmem),
        name="ffn_final" if final_norm else "ffn",
    )(x2d, norm_g, w_gate, w_up, w_down, final_g)


def _iota2(shape, axis):
    return lax.broadcasted_iota(jnp.int32, shape, axis)


def _segsum(x, lane_lo):
    outs = []
    for j in range(x.shape[1] // LANES):
        t = x[:, j * LANES:(j + 1) * LANES]
        lo = jnp.sum(jnp.where(lane_lo, t, 0.0), axis=-1, keepdims=True)
        hi = jnp.sum(jnp.where(lane_lo, 0.0, t), axis=-1, keepdims=True)
        outs.append(jnp.where(lane_lo, lo, hi))
    return jnp.concatenate(outs, axis=1)


def _block_rows(x, idx, fn=None):
    c = CHUNK
    outs = []
    for r in range(0, x.shape[0], c):
        row = x[r + idx:r + idx + 1]
        if fn is not None:
            row = fn(row, x[r + c // 2:r + c // 2 + 1])
        outs.append(jnp.broadcast_to(row, (c, x.shape[1])))
    return jnp.concatenate(outs, axis=0)


def _tile(x, b, j, width):
    return x[b * CHUNK:(b + 1) * CHUNK, j * width:(j + 1) * width]


def _assemble(tiles, nb, nj):
    return jnp.concatenate(
        [jnp.concatenate([tiles[(b, j)] for j in range(nj)], axis=1) for b in range(nb)], axis=0)


def _merge_stages(*stage_lists):
    tagged = []
    for stages in stage_lists:
        tagged += [((i + 0.5) / len(stages), i, fn) for i, fn in enumerate(stages)]
    for _, _, fn in sorted(tagged, key=lambda t: t[:2]):
        fn()


class _Masks:
    def __init__(self, rows):
        c = CHUNK
        ri, ci = _iota2((c, c), 0), _iota2((c, c), 1)
        self.causal = ci <= ri
        bi, bj = _iota2((rows, rows), 0), _iota2((rows, rows), 1)
        same_block = lax.shift_right_logical(bi, 6) == lax.shift_right_logical(bj, 6)
        tri = ((bj <= bi) & same_block).astype(BF16)
        self.tri2 = jnp.concatenate([tri, tri], axis=1)
        gi, gj = _iota2((2 * c, 4 * c), 0), _iota2((2 * c, 4 * c), 1)
        ti, sj = gi & (c - 1), gj & (c - 1)
        self.gmask = (sj < ti) | ((gi >= c) & (sj == ti))
        li, lj = _iota2((LANES, LANES), 0), _iota2((LANES, LANES), 1)
        self.bdmask = lax.shift_right_logical(li, 6) == lax.shift_right_logical(lj, 6)
        self.lane_lo = _iota2((1, LANES), 1) < HB_HEAD_DIM
        self.row0 = (_iota2((rows, 1), 0) & (c - 1)) == 0
        self.lora_lane = _iota2((1, LORA_PAD), 1)


def _hgrn_stages(ctx, hs_ref, b0, nb, w_a, mk):
    hd = HA_HEAD_DIM
    chains = [(b, h) for b in range(nb) for h in range(w_a // hd)]
    st = {}

    def scores():
        qt, kt = ctx['hg'][0], ctx['hg'][1]
        st['s_old'] = {ch: hs_ref[b0 + ch[0], ch[1]] for ch in chains}
        st['sc'] = {ch: jnp.where(mk.causal, _dot_nt(_tile(qt, *ch, hd), _tile(kt, *ch, hd)), 0.0).astype(BF16)
                    for ch in chains}

    def outputs():
        _, _, kl, q0, v = ctx['hg']
        s_old = st['s_old']
        o = {ch: _dot(st['sc'][ch], _tile(v, *ch, hd)) + _dot_nt(_tile(q0, *ch, hd), s_old[ch].astype(BF16))
             for ch in chains}
        for ch in chains:
            d = ctx['hg_d'][ch[0]][:, ch[1] * hd:(ch[1] + 1) * hd]
            hs_ref[b0 + ch[0], ch[1]] = d * s_old[ch] + _dot_tn(_tile(v, *ch, hd), _tile(kl, *ch, hd))
        st['o'] = o

    def normalise():
        o = {ch: t * lax.rsqrt(jnp.mean(t * t, axis=-1, keepdims=True) + NORM_EPS) for ch, t in st['o'].items()}
        ctx['o_a'] = _assemble(o, nb, w_a // hd)

    return [scores, outputs, normalise]


def _rwkv_stages(ctx, rs_ref, b0, nb, w_b, mk):
    c = CHUNK
    n_pairs = w_b // LANES
    lane_lo = mk.lane_lo
    pairs = [(b, pr) for b in range(nb) for pr in range(n_pairs)]
    tl = lambda t, pair: _tile(t, pair[0], pair[1], LANES)
    row = lambda name, pair: ctx[name][pair[0]][:, pair[1] * LANES:(pair[1] + 1) * LANES]
    zero = jnp.zeros((c, LANES), BF16)
    lo = lambda t: jnp.where(lane_lo, t, zero)
    hi = lambda t: jnp.where(lane_lo, zero, t)
    cat0 = lambda *ts: jnp.concatenate(ts, axis=0)
    st = {}

    def state_read():
        at, rt, bt, kt, v = ctx['rw']
        st['s_old'] = {pp: rs_ref[b0 + pp[0], pp[1]] for pp in pairs}
        st['z'] = {pp: _dot_nt(cat0(tl(at, pp), tl(rt, pp)),
                               (st['s_old'][pp] * row('rw_ref', pp)).astype(BF16)) for pp in pairs}

    def gram():
        at, rt, bt, kt, v = ctx['rw']
        g = {}
        for pp in pairs:
            b_, k_ = tl(bt, pp), tl(kt, pp)
            g[pp] = jnp.where(
                mk.gmask, _dot_nt(cat0(tl(at, pp), tl(rt, pp)), cat0(lo(b_), hi(b_), lo(k_), hi(k_))), 0.0)
        st['g'] = g

    def rhs0():
        v = ctx['rw'][4]
        x, pw = {}, {}
        for pp in pairs:
            pw[pp] = st['g'][pp][:c, :LANES].astype(BF16)
            lak = st['g'][pp][:c, LANES:].astype(BF16)
            x[pp] = st['z'][pp][:c] + _dot(lak, cat0(lo(tl(v, pp)), hi(tl(v, pp))))
        st['x'], st['pw'] = x, pw

    def solve_step():
        x, pw, res = st['x'], st['pw'], {}
        for pp in pairs:
            xb = x[pp].astype(BF16)
            rhs = jnp.concatenate([cat0(lo(xb), hi(xb)), cat0(lo(pw[pp]), hi(pw[pp]))], axis=1)
            res[pp] = _dot(pw[pp], rhs)
        st['x'] = {pp: x[pp] + res[pp][:, :LANES] for pp in pairs}
        st['pw'] = {pp: res[pp][:, LANES:].astype(BF16) for pp in pairs}

    def solve_last():
        x, pw = st['x'], st['pw']
        for pp in pairs:
            xb = x[pp].astype(BF16)
            x[pp] = x[pp] + _dot(pw[pp], cat0(lo(xb), hi(xb)))

    def outputs():
        at, rt, bt, kt, v = ctx['rw']
        y = {}
        for pp in pairs:
            ub, vv = st['x'][pp].astype(BF16), tl(v, pp)
            y[pp] = st['z'][pp][c:] + _dot(st['g'][pp][c:].astype(BF16), cat0(lo(ub), hi(ub), lo(vv), hi(vv)))
            upd = _dot_tn(cat0(ub, vv), cat0(tl(bt, pp), tl(kt, pp)))
            rs_ref[b0 + pp[0], pp[1]] = (row('rw_dl', pp) * st['s_old'][pp]
                                         + jnp.where(mk.bdmask, upd, 0.0) * row('rw_last', pp))
        ctx['y'] = _assemble(y, nb, n_pairs)

    n_steps = 0
    n = 1
    while 2 * n < c:
        n_steps += 1
        n *= 2
    return [state_read, gram, rhs0] + [solve_step] * n_steps + [solve_last, outputs]


HG_OPS, RW_OPS = 5, 5
N_F32 = 3
N_ROWS = 4


def _mix_in_kernel(x_ref, mn_ref, win_ref, lbl_ref, hon_ref, mu_ref, w0_ref, a0_ref, wcomb_ref,
                   kk_ref, ka_ref, rk_ref,
                   ob0_ref, of0_ref, or0_ref, ob1_ref, of1_ref, or1_ref,
                   p_ref, carry_ref, *, nb, w_a, w_b):
    c = CHUNK
    rows = nb * c
    d_model = x_ref.shape[-1]
    n_rkv = 3 * w_b + LORA_PAD
    outs = ((ob0_ref, of0_ref, or0_ref), (ob1_ref, of1_ref, or1_ref))

    @pl.when(pl.program_id(0) == 0)
    def _():
        carry_ref[...] = jnp.zeros_like(carry_ref)
        p_ref[rows:2 * rows, :] = jnp.zeros((rows, p_ref.shape[1]), F32)

    mk = _Masks(c)
    logits = lbl_ref[...]
    e = jnp.exp(logits - jnp.max(logits, axis=0, keepdims=True))
    lb = e[0:1] / jnp.sum(e, axis=0, keepdims=True)

    def in_proj_stages(grp):
        prow = slice(grp * rows, (grp + 1) * rows)
        hn = {}

        def norm():
            x = x_ref[grp * nb:(grp + 1) * nb].reshape(rows, d_model)
            hn['h'] = _rms_norm(x, mn_ref[...]).astype(BF16)

        def cols(lo, hi):
            def project():
                p_ref[prow, lo:hi] = _dot(hn['h'], win_ref[:, lo:hi])
            return project

        n_cols = 4 * w_a + n_rkv
        return [norm] + [cols(lo, min(lo + IN_PROJ_COL_TILE, n_cols))
                         for lo in range(0, n_cols, IN_PROJ_COL_TILE)]

    def prep_stages(grp):
        per_batch = [prep_batch_stages(grp, b) for b in range(nb)]
        return [fn for stage in zip(*per_batch) for fn in stage]

    def prep_batch_stages(grp, b):
        bidx = grp * nb + b
        prow = slice(bidx * c, (bidx + 1) * c)
        ob_ref, of_ref, or_ref = outs[grp]
        ctx = {}

        def put_ops(first, arrays):
            for i, t in enumerate(arrays):
                ob_ref[b, :, (first + i) * w_b:(first + i + 1) * w_b] = t.astype(BF16)

        def put_rows(first, arrays):
            for i, t in enumerate(arrays):
                or_ref[0, b, :, (first + i) * w_b:(first + i + 1) * w_b] = jnp.broadcast_to(t, (8, w_b))

        def hgrn_gates():
            q_a = p_ref[prow, 0:w_a]
            f_a = p_ref[prow, w_a:2 * w_a]
            forget = lb + (1.0 - lb) * _sigmoid(f_a)
            ctx['hg_logf'] = _split_bf16(jnp.log(forget))
            ctx['hg_kh'] = 1.0 - forget
            ctx['hg_q'] = q_a * _sigmoid(q_a)
            g_a = p_ref[prow, 3 * w_a:4 * w_a]
            of_ref[b, :, 0:w_a] = hon_ref[...] * (g_a * _sigmoid(g_a))

        def hgrn_decays():
            hi, lo = ctx['hg_logf']
            bc = _dot(mk.tri2, jnp.concatenate([hi, lo], axis=0))
            q, kh = ctx['hg_q'], ctx['hg_kh']
            b_ref = _block_rows(bc, c // 2)
            qt = q * jnp.exp(bc - b_ref)
            kt = kh * jnp.exp(b_ref - bc)
            kl = kt * _block_rows(bc, c - 1, lambda last, ref: jnp.exp(last - ref))
            q0 = qt * _block_rows(bc, c - 1, lambda last, ref: jnp.exp(ref))
            put_ops(0, (qt, kt, kl, q0, p_ref[prow, 2 * w_a:3 * w_a]))
            put_rows(0, (jnp.exp(bc[c - 1:c]),))

        def rwkv_shift():
            pb = p_ref[prow, 4 * w_a:4 * w_a + n_rkv]
            prev = jnp.broadcast_to(carry_ref[bidx:bidx + 1, :], (c, n_rkv))
            shifted = jnp.where(mk.row0, prev, pltpu.roll(pb, 1, 0))
            carry_ref[bidx:bidx + 1, :] = pb[c - 1:c, :]
            pb = pb + mu_ref[...] * (shifted - pb)
            ctx['r'] = pb[:, 0:w_b]
            ctx['k'] = pb[:, w_b:2 * w_b]
            ctx['v'] = pb[:, 2 * w_b:3 * w_b]
            low = pb[:, 3 * w_b:]
            ll = mk.lora_lane
            act = jnp.where(ll < DECAY_LORA, jnp.tanh(low),
                            jnp.where(ll < DECAY_LORA + AAA_LORA, low,
                                      jnp.where(ll < DECAY_LORA + AAA_LORA + GATE_LORA, _sigmoid(low), 0.0)))
            ctx['act'] = act.astype(BF16)

        def rwkv_lora():
            r, k = ctx['r'], ctx['k']
            ld = _dot(ctx['act'], wcomb_ref[...])
            lw = -DECAY_SCALE * _sigmoid(w0_ref[...] + ld[:, 0:w_b])
            a = _sigmoid(a0_ref[...] + ld[:, w_b:2 * w_b])
            ctx['gate'] = ld[:, 2 * w_b:3 * w_b]
            kk = k * kk_ref[...]
            k = k * (1.0 + (a - 1.0) * ka_ref[...])
            ctx['k'], ctx['a'], ctx['kk'], ctx['lw'] = k, a, kk, lw
            ctx['lw_split'] = _split_bf16(lw)
            ctx['kk_sq'] = kk * kk
            ctx['rk'] = r * k * rk_ref[...]

        def rwkv_sums():
            hi, lo = ctx['lw_split']
            ctx['gc'] = _dot(mk.tri2, jnp.concatenate([hi, lo], axis=0))
            kk = ctx['kk'] * lax.rsqrt(jnp.maximum(_segsum(ctx['kk_sq'], mk.lane_lo), L2_EPS * L2_EPS))
            of_ref[b, :, w_b:2 * w_b] = _segsum(ctx['rk'], mk.lane_lo) * ctx['v']
            of_ref[b, :, 2 * w_b:3 * w_b] = ctx['gate']
            ctx['av'], ctx['bv'] = -kk, kk * ctx['a']

        def rwkv_decays():
            r, k, v, av, bv, lw, gc = (ctx[n] for n in ('r', 'k', 'v', 'av', 'bv', 'lw', 'gc'))
            gref = _block_rows(gc, c // 2)
            e_dn = jnp.exp(gref - gc)
            put_ops(HG_OPS, (av * jnp.exp(gc - lw - gref), r * jnp.exp(gc - gref), bv * e_dn, k * e_dn, v))
            g_mid, g_last = gc[c // 2:c // 2 + 1], gc[c - 1:c]
            put_rows(1, (jnp.exp(g_mid), jnp.exp(g_last - g_mid), jnp.exp(g_last)))

        return [hgrn_gates, hgrn_decays, rwkv_shift, rwkv_lora, rwkv_sums, rwkv_decays]

    _merge_stages(in_proj_stages(0), prep_stages(1))
    _merge_stages(in_proj_stages(1), prep_stages(0))


def _mix_out_kernel(ob0_ref, of0_ref, or0_ref, ob1_ref, of1_ref, or1_ref, x_ref, gnw_ref, gnb_ref, wout_ref,
                    o_ref, hs_ref, rs_ref, y_ref, oa_ref, *, nb, w_a, w_b):
    c = CHUNK
    rows = nb * c
    d_model = x_ref.shape[-1]
    ins = ((ob0_ref, of0_ref, or0_ref), (ob1_ref, of1_ref, or1_ref))
    n_groups = len(ins)

    @pl.when(pl.program_id(0) == 0)
    def _():
        hs_ref[...] = jnp.zeros_like(hs_ref)
        rs_ref[...] = jnp.zeros_like(rs_ref)
        y_ref[...] = jnp.zeros_like(y_ref)
        oa_ref[...] = jnp.zeros_like(oa_ref)

    mk = _Masks(c)
    y_prev, oa_prev = y_ref[...], oa_ref[...]

    def chain_stages(grp, ctx):
        ob_ref, _, or_ref = ins[grp]
        b0 = grp * nb

        def load():
            ops = [ob_ref[:, :, i * w_b:(i + 1) * w_b].reshape(rows, w_b) for i in range(HG_OPS + RW_OPS)]
            ctx['hg'], ctx['rw'] = tuple(ops[:HG_OPS]), tuple(ops[HG_OPS:])
            rws = lambda i: [or_ref[0, b, 0:1, i * w_b:(i + 1) * w_b] for b in range(nb)]
            ctx['hg_d'], ctx['rw_ref'], ctx['rw_last'], ctx['rw_dl'] = rws(0), rws(1), rws(2), rws(3)

        return ([load] + _hgrn_stages(ctx, hs_ref, b0, nb, w_a, mk)
                + _rwkv_stages(ctx, rs_ref, b0, nb, w_b, mk))

    def post_stages(grp, parts):
        _, of_ref, _ = ins[grp]
        inv_n = 1.0 / HB_HEAD_DIM

        def batch_stages(b, st):
            brow = slice((grp * nb + b) * c, (grp * nb + b + 1) * c)

            def gate_a():
                st['o_a'] = (oa_prev[brow] * of_ref[b, :, 0:w_a]).astype(BF16)
                st['y'] = y_prev[brow]

            def gn_mean():
                yc = st['y'] - _segsum(st['y'], mk.lane_lo) * inv_n
                st['yc'], st['yc_sq'] = yc, yc * yc

            def gn_var():
                var = _segsum(st['yc_sq'], mk.lane_lo) * inv_n
                yn = st['yc'] * lax.rsqrt(var + RWKV_GN_EPS) * gnw_ref[...] + gnb_ref[...]
                o_b = ((yn + of_ref[b, :, w_b:2 * w_b]) * of_ref[b, :, 2 * w_b:3 * w_b]).astype(BF16)
                st['o'] = jnp.concatenate([st['o_a'], o_b], axis=1)

            return [gate_a, gn_mean, gn_var]

        per_batch = [batch_stages(b, parts[grp * nb + b]) for b in range(nb)]
        return [fn for stage in zip(*per_batch) for fn in stage]

    def out_proj_stages(parts):
        def out_proj():
            o = jnp.concatenate([st['o'] for st in parts], axis=0)
            x = x_ref[...].reshape(n_groups * rows, d_model)
            o_ref[...] = (x + _dot(o, wout_ref[...])).reshape(n_groups * nb, c, d_model)
        return [out_proj]

    ctxs = [dict() for _ in range(n_groups)]
    parts = [dict() for _ in range(n_groups * nb)]
    _merge_stages(chain_stages(0, ctxs[0]), post_stages(0, parts))
    _merge_stages(chain_stages(1, ctxs[1]), post_stages(1, parts) + out_proj_stages(parts))
    for grp in range(n_groups):
        y_ref[grp * rows:(grp + 1) * rows, :] = ctxs[grp]['y']
        oa_ref[grp * rows:(grp + 1) * rows, :] = ctxs[grp]['o_a']


def _mixer(x, mix_norm, w_in, lb_logits, hgrn_out_norm, mu, w0, w2, a0, a2, g2, k_k, k_a, r_k,
           gn_w, gn_b, w_out):
    bsz, t, d = x.shape
    w_a = hgrn_out_norm.shape[-1]
    w_b = w0.shape[-1]
    nb, c = MIXER_GROUP, CHUNK
    nt = t // c
    n_lora = DECAY_LORA + AAA_LORA + GATE_LORA
    assert bsz == 2 * nb and t % c == 0 and w_a == w_b
    assert w_a % HA_HEAD_DIM == 0 and w_b % LANES == 0
    assert w_in.shape[1] == 4 * w_a + 3 * w_b + n_lora
    pad = LORA_PAD - n_lora
    n_cols = 4 * w_a + 3 * w_b + LORA_PAD
    n_rkv = 3 * w_b + LORA_PAD
    w_in_p = jnp.pad(w_in, ((0, 0), (0, pad))).astype(BF16)
    mu_p = jnp.pad(mu.reshape(1, -1), ((0, 0), (0, pad)))
    wcomb = jnp.zeros((LORA_PAD, 3 * w_b), F32)
    wcomb = wcomb.at[0:DECAY_LORA, 0:w_b].set(w2)
    wcomb = wcomb.at[DECAY_LORA:DECAY_LORA + AAA_LORA, w_b:2 * w_b].set(a2)
    wcomb = wcomb.at[DECAY_LORA + AAA_LORA:n_lora, 2 * w_b:3 * w_b].set(g2)
    row = lambda vec: vec.reshape(1, -1).astype(F32)
    n_pairs = w_b // LANES
    n_heads_a = w_a // HA_HEAD_DIM
    n_ops = HG_OPS + RW_OPS

    last = nt - 1
    ob_shape = jax.ShapeDtypeStruct((nb, t + c, n_ops * w_b), BF16)
    of_shape = jax.ShapeDtypeStruct((nb, t + c, N_F32 * w_b), F32)
    or_shape = jax.ShapeDtypeStruct((nt + 1, nb, 8, N_ROWS * w_b), F32)
    g0_idx = lambda j: (0, j, 0)
    g1_idx = lambda j: (0, jnp.maximum(j - 1, 0), 0)
    in_vmem = ((d * n_cols + LORA_PAD * 3 * w_b) * 2 + 2 * 2 * nb * c * n_cols * 4
               + 2 * 2 * nb * c * (d * 4 + n_ops * w_b * 2 + N_F32 * w_b * 4) + 8 * MIB)
    handoff = pl.pallas_call(
        functools.partial(_mix_in_kernel, nb=nb, w_a=w_a, w_b=w_b),
        grid=(nt + 1,),
        in_specs=[
            pl.BlockSpec((2 * nb, c, d), lambda j: (0, jnp.minimum(j, last), 0)),
            _resident((1, d)),
            _resident((d, n_cols)),
            _resident(lb_logits.shape),
            _resident((1, w_a)),
            _resident((1, n_rkv)),
            _resident((1, w_b)),
            _resident((1, w_b)),
            _resident((LORA_PAD, 3 * w_b)),
            _resident((1, w_b)),
            _resident((1, w_b)),
            _resident((1, w_b)),
        ],
        out_specs=[
            pl.BlockSpec((nb, c, n_ops * w_b), g0_idx),
            pl.BlockSpec((nb, c, N_F32 * w_b), g0_idx),
            pl.BlockSpec((1, nb, 8, N_ROWS * w_b), lambda j: (j, 0, 0, 0)),
            pl.BlockSpec((nb, c, n_ops * w_b), g1_idx),
            pl.BlockSpec((nb, c, N_F32 * w_b), g1_idx),
            pl.BlockSpec((1, nb, 8, N_ROWS * w_b), lambda j: (jnp.maximum(j - 1, 0), 0, 0, 0)),
        ],
        out_shape=[ob_shape, of_shape, or_shape, ob_shape, of_shape, or_shape],
        scratch_shapes=[
            pltpu.VMEM((2 * nb * c, n_cols), F32),
            pltpu.VMEM((2 * nb, n_rkv), F32),
        ],
        compiler_params=pltpu.CompilerParams(dimension_semantics=("arbitrary",), vmem_limit_bytes=in_vmem),
        name="mix_in",
    )(x, row(mix_norm), w_in_p, lb_logits.astype(F32), row(hgrn_out_norm), mu_p, row(w0), row(a0),
      wcomb.astype(BF16), row(k_k), row(k_a), row(r_k))

    cur = lambda j: (0, jnp.minimum(j, last), 0)
    prev = lambda j: (0, jnp.maximum(j - 1, 0), 0)
    state_bytes = 2 * nb * (n_heads_a * HA_HEAD_DIM * HA_HEAD_DIM + n_pairs * LANES * LANES) * 4
    out_vmem = ((w_a + w_b) * d * 2 + state_bytes + 2 * 2 * nb * c * (w_a + w_b) * 4
                + 2 * 2 * nb * c * (2 * d * 4 + n_ops * w_b * 2 + N_F32 * w_b * 4) + 12 * MIB)
    group_specs = [
        pl.BlockSpec((nb, c, n_ops * w_b), cur),
        pl.BlockSpec((nb, c, N_F32 * w_b), prev),
        pl.BlockSpec((1, nb, 8, N_ROWS * w_b), lambda j: (jnp.minimum(j, last), 0, 0, 0)),
    ]
    return pl.pallas_call(
        functools.partial(_mix_out_kernel, nb=nb, w_a=w_a, w_b=w_b),
        grid=(nt + 1,),
        in_specs=group_specs + group_specs + [
            pl.BlockSpec((2 * nb, c, d), prev),
            _resident((1, w_b)),
            _resident((1, w_b)),
            _resident((w_a + w_b, d)),
        ],
        out_specs=pl.BlockSpec((2 * nb, c, d), prev),
        out_shape=jax.ShapeDtypeStruct((bsz, t, d), F32),
        scratch_shapes=[
            pltpu.VMEM((2 * nb, n_heads_a, HA_HEAD_DIM, HA_HEAD_DIM), F32),
            pltpu.VMEM((2 * nb, n_pairs, LANES, LANES), F32),
            pltpu.VMEM((2 * nb * c, w_b), F32),
            pltpu.VMEM((2 * nb * c, w_a), F32),
        ],
        compiler_params=pltpu.CompilerParams(dimension_semantics=("arbitrary",), vmem_limit_bytes=out_vmem),
        name="mix_out",
    )(*handoff, x, row(gn_w), row(gn_b), w_out.astype(BF16))


def kernel(x, ffn1_norm, ffn1_w_gate, ffn1_w_up, ffn1_w_down, mix_norm, w_in, hgrn_lb_logits, hgrn_out_norm, rwkv_shift_mu, rwkv_w0, rwkv_w2, rwkv_a0, rwkv_a2, rwkv_g2, rwkv_k_k, rwkv_k_a, rwkv_r_k, rwkv_gn_w, rwkv_gn_b, w_out, ffn2_norm, ffn2_w_gate, ffn2_w_up, ffn2_w_down, final_norm):
    bsz, t, d = x.shape
    depth = ffn1_norm.shape[0]
    assert depth == 1 and hgrn_lb_logits.shape[0] == depth + 1
    l = 0
    row = lambda vec: vec.reshape(1, -1).astype(F32)
    fg = row(final_norm)
    h = _ffn(x.reshape(bsz * t, d), row(ffn1_norm[l]), ffn1_w_gate[l].astype(BF16),
             ffn1_w_up[l].astype(BF16), ffn1_w_down[l].astype(BF16), fg, final_norm=False)
    h = _mixer(h.reshape(bsz, t, d), mix_norm[l], w_in[l], hgrn_lb_logits, hgrn_out_norm[l],
               rwkv_shift_mu[l], rwkv_w0[l], rwkv_w2[l], rwkv_a0[l], rwkv_a2[l], rwkv_g2[l],
               rwkv_k_k[l], rwkv_k_a[l], rwkv_r_k[l], rwkv_gn_w[l], rwkv_gn_b[l], w_out[l])
    h = _ffn(h.reshape(bsz * t, d), row(ffn2_norm[l]), ffn2_w_gate[l].astype(BF16),
             ffn2_w_up[l].astype(BF16), ffn2_w_down[l].astype(BF16), fg, final_norm=True)
    return h.reshape(bsz, t, d)
```

```python
import functools
import math

import jax
import jax.numpy as jnp
from jax import lax
from jax.experimental import pallas as pl
from jax.experimental.pallas import tpu as pltpu

F32 = jnp.float32
BF16 = jnp.bfloat16

NORM_EPS = 1e-6
RWKV_GN_EPS = 64e-5
L2_EPS = 1e-12
DECAY_SCALE = math.exp(-0.5)

LANES = 128
MXU_COLS = 256
HA_HEAD_DIM = 128
HB_HEAD_DIM = 64
CHUNK = 64
DECAY_LORA, AAA_LORA, GATE_LORA = 32, 32, 96
LORA_PAD = 256
FFN_ROW_TILE = 1024
FFN_COL_TILE = 768
FFN_WEIGHT_SLABS = 8
MIXER_GROUP = 4
MIXER_GROUPS_PER_STEP = 2
IN_PROJ_COL_TILE = 512
MIB = 1024 * 1024


def _dot(a, b):
    return jnp.dot(a, b, preferred_element_type=F32)


def _dot_nt(a, b):
    return lax.dot_general(a, b, (((1,), (1,)), ((), ())), preferred_element_type=F32)


def _dot_tn(a, b):
    return lax.dot_general(a, b, (((0,), (0,)), ((), ())), preferred_element_type=F32)


def _split_bf16(x):
    hi = x.astype(BF16)
    lo = (x - hi.astype(F32)).astype(BF16)
    return hi, lo


def _rms_norm(x, g):
    return x * lax.rsqrt(jnp.mean(x * x, axis=-1, keepdims=True) + NORM_EPS) * g


def _sigmoid(x):
    return 0.5 * jnp.tanh(0.5 * x) + 0.5


def _ffn_kernel(x_ref, g_ref, wg_ref, wu_ref, wd_ref, fn_ref, o_ref, wg_s, wu_s, wd_s, *,
                col_tiles, final_norm, n_load):
    i = pl.program_id(0)

    @pl.when(i < n_load)
    def _():
        rg, rd = wg_ref.shape[0], wd_ref.shape[0]
        wg_s[pl.ds(pl.multiple_of(i * rg, rg), rg), :] = wg_ref[...].astype(BF16)
        wu_s[pl.ds(pl.multiple_of(i * rg, rg), rg), :] = wu_ref[...].astype(BF16)
        wd_s[pl.ds(pl.multiple_of(i * rd, rd), rd), :] = wd_ref[...].astype(BF16)

    @pl.when(i >= n_load)
    def _():
        x = x_ref[...]
        h = _rms_norm(x, g_ref[...]).astype(BF16)
        acc = jnp.zeros(x.shape, F32)
        for lo, hi in col_tiles:
            gate = _dot(h, wg_s[:, lo:hi])
            up = _dot(h, wu_s[:, lo:hi])
            act = (gate * _sigmoid(gate) * up).astype(BF16)
            acc = acc + _dot(act, wd_s[lo:hi, :])
        y = x + 0.5 * acc
        if final_norm:
            y = _rms_norm(y, fn_ref[...])
        o_ref[...] = y


def _resident(shape):
    return pl.BlockSpec(shape, lambda *_: (0,) * len(shape), pipeline_mode=pl.Buffered(1))


def _ffn(x2d, norm_g, w_gate, w_up, w_down, final_g, *, final_norm):
    n, d = x2d.shape
    d_ff = w_gate.shape[1]
    tm, n_load = FFN_ROW_TILE, FFN_WEIGHT_SLABS
    assert n % tm == 0 and d_ff % MXU_COLS == 0
    assert d % (16 * n_load) == 0 and d_ff % (16 * n_load) == 0
    col_tiles = tuple((lo, min(lo + FFN_COL_TILE, d_ff)) for lo in range(0, d_ff, FFN_COL_TILE))
    weight_bytes = 3 * d * d_ff * 2
    slab_bytes = 3 * d * d_ff * 4 // n_load
    tile_bytes = tm * d * 4
    vmem = weight_bytes + 2 * slab_bytes + 8 * tile_bytes + 3 * tm * FFN_COL_TILE * 4 + 8 * MIB
    slab = lambda i: (jnp.minimum(i, n_load - 1), 0)
    tile = lambda i: (jnp.maximum(i - n_load, 0), 0)
    return pl.pallas_call(
        functools.partial(_ffn_kernel, col_tiles=col_tiles, final_norm=final_norm, n_load=n_load),
        grid=(n_load + n // tm,),
        in_specs=[
            pl.BlockSpec((tm, d), tile),
            _resident((1, d)),
            pl.BlockSpec((d // n_load, d_ff), slab),
            pl.BlockSpec((d // n_load, d_ff), slab),
            pl.BlockSpec((d_ff // n_load, d), slab),
            _resident((1, d)),
        ],
        out_specs=pl.BlockSpec((tm, d), tile),
        out_shape=jax.ShapeDtypeStruct((n, d), F32),
        scratch_shapes=[
            pltpu.VMEM((d, d_ff), BF16),
            pltpu.VMEM((d, d_ff), BF16),
            pltpu.VMEM((d_ff, d), BF16),
        ],
        compiler_params=pltpu.CompilerParams(
            dimension_semantics=("arbitrary",), vmem_limit_bytes=vmem),
        name="ffn_final" if final_norm else "ffn",
    )(x2d, norm_g, w_gate, w_up, w_down, final_g)


def _iota2(shape, axis):
    return lax.broadcasted_iota(jnp.int32, shape, axis)


def _segsum(x, lane_lo):
    outs = []
    for j in range(x.shape[1] // LANES):
        t = x[:, j * LANES:(j + 1) * LANES]
        lo = jnp.sum(jnp.where(lane_lo, t, 0.0), axis=-1, keepdims=True)
        hi = jnp.sum(jnp.where(lane_lo, 0.0, t), axis=-1, keepdims=True)
        outs.append(jnp.where(lane_lo, lo, hi))
    return jnp.concatenate(outs, axis=1)


def _block_rows(x, idx, fn=None):
    c = CHUNK
    outs = []
    for r in range(0, x.shape[0], c):
        row = x[r + idx:r + idx + 1]
        if fn is not None:
            row = fn(row, x[r + c // 2:r + c // 2 + 1])
        outs.append(jnp.broadcast_to(row, (c, x.shape[1])))
    return jnp.concatenate(outs, axis=0)


def _tile(x, b, j, width):
    return x[b * CHUNK:(b + 1) * CHUNK, j * width:(j + 1) * width]


def _assemble(tiles, nb, nj):
    return jnp.concatenate(
        [jnp.concatenate([tiles[(b, j)] for j in range(nj)], axis=1) for b in range(nb)], axis=0)


def _merge_stages(*stage_lists):
    tagged = []
    for stages in stage_lists:
        tagged += [((i + 0.5) / len(stages), i, fn) for i, fn in enumerate(stages)]
    for _, _, fn in sorted(tagged, key=lambda t: t[:2]):
        fn()


class _Masks:
    def __init__(self, rows):
        c = CHUNK
        ri, ci = _iota2((c, c), 0), _iota2((c, c), 1)
        self.causal = ci <= ri
        bi, bj = _iota2((rows, rows), 0), _iota2((rows, rows), 1)
        same_block = lax.shift_right_logical(bi, 6) == lax.shift_right_logical(bj, 6)
        tri = ((bj <= bi) & same_block).astype(BF16)
        self.tri2 = jnp.concatenate([tri, tri], axis=1)
        gi, gj = _iota2((2 * c, 4 * c), 0), _iota2((2 * c, 4 * c), 1)
        ti, sj = gi & (c - 1), gj & (c - 1)
        self.gmask = (sj < ti) | ((gi >= c) & (sj == ti))
        li, lj = _iota2((LANES, LANES), 0), _iota2((LANES, LANES), 1)
        self.bdmask = lax.shift_right_logical(li, 6) == lax.shift_right_logical(lj, 6)
        self.lane_lo = _iota2((1, LANES), 1) < HB_HEAD_DIM
        self.row0 = (_iota2((rows, 1), 0) & (c - 1)) == 0
        self.lora_lane = _iota2((1, LORA_PAD), 1)


def _hgrn_stages(ctx, hs_ref, b0, nb, w_a, mk):
    hd = HA_HEAD_DIM
    chains = [(b, h) for b in range(nb) for h in range(w_a // hd)]
    st = {}

    def scores():
        qt, kt = ctx['hg_qt'], ctx['hg_kt']
        st['s_old'] = {ch: hs_ref[b0 + ch[0], ch[1]] for ch in chains}
        st['sc'] = {ch: jnp.where(mk.causal, _dot_nt(_tile(qt, *ch, hd), _tile(kt, *ch, hd)), 0.0).astype(BF16)
                    for ch in chains}

    def outputs():
        v, q0, kl, d = ctx['hg_v'], ctx['hg_q0'], ctx['hg_kl'], ctx['hg_d']
        s_old = st['s_old']
        o = {ch: _dot(st['sc'][ch], _tile(v, *ch, hd)) + _dot_nt(_tile(q0, *ch, hd), s_old[ch].astype(BF16))
             for ch in chains}
        for ch in chains:
            hs_ref[b0 + ch[0], ch[1]] = (_tile(d, *ch, hd)[0:1] * s_old[ch]
                                         + _dot_tn(_tile(v, *ch, hd), _tile(kl, *ch, hd)))
        st['o'] = o

    def normalise():
        o = {ch: t * lax.rsqrt(jnp.mean(t * t, axis=-1, keepdims=True) + NORM_EPS) for ch, t in st['o'].items()}
        ctx['o_a'] = _assemble(o, nb, w_a // hd)

    return [scores, outputs, normalise]


def _rwkv_stages(ctx, rs_ref, b0, nb, w_b, mk):
    c = CHUNK
    n_pairs = w_b // LANES
    lane_lo = mk.lane_lo
    pairs = [(b, pr) for b in range(nb) for pr in range(n_pairs)]
    tl = lambda t, pair: _tile(t, pair[0], pair[1], LANES)
    zero = jnp.zeros((c, LANES), BF16)
    lo = lambda t: jnp.where(lane_lo, t, zero)
    hi = lambda t: jnp.where(lane_lo, zero, t)
    cat0 = lambda *ts: jnp.concatenate(ts, axis=0)
    st = {}

    def state_read():
        at, rt, bt, kt, a0, r0, bl, kl, v = ctx['rw']
        st['s_old'] = {pp: rs_ref[b0 + pp[0], pp[1]] for pp in pairs}
        st['z'] = {pp: _dot_nt(cat0(tl(a0, pp), tl(r0, pp)), st['s_old'][pp].astype(BF16)) for pp in pairs}

    def gram():
        at, rt, bt, kt, a0, r0, bl, kl, v = ctx['rw']
        g = {}
        for pp in pairs:
            b_, k_ = tl(bt, pp), tl(kt, pp)
            g[pp] = jnp.where(
                mk.gmask, _dot_nt(cat0(tl(at, pp), tl(rt, pp)), cat0(lo(b_), hi(b_), lo(k_), hi(k_))), 0.0)
        st['g'] = g

    def rhs0():
        v = ctx['rw'][8]
        x, pw = {}, {}
        for pp in pairs:
            pw[pp] = st['g'][pp][:c, :LANES].astype(BF16)
            lak = st['g'][pp][:c, LANES:].astype(BF16)
            x[pp] = st['z'][pp][:c] + _dot(lak, cat0(lo(tl(v, pp)), hi(tl(v, pp))))
        st['x'], st['pw'] = x, pw

    def solve_step():
        x, pw, res = st['x'], st['pw'], {}
        for pp in pairs:
            xb = x[pp].astype(BF16)
            rhs = jnp.concatenate([cat0(lo(xb), hi(xb)), cat0(lo(pw[pp]), hi(pw[pp]))], axis=1)
            res[pp] = _dot(pw[pp], rhs)
        st['x'] = {pp: x[pp] + res[pp][:, :LANES] for pp in pairs}
        st['pw'] = {pp: res[pp][:, LANES:].astype(BF16) for pp in pairs}

    def solve_last():
        x, pw = st['x'], st['pw']
        for pp in pairs:
            xb = x[pp].astype(BF16)
            x[pp] = x[pp] + _dot(pw[pp], cat0(lo(xb), hi(xb)))

    def outputs():
        at, rt, bt, kt, a0, r0, bl, kl, v = ctx['rw']
        dl = ctx['rw_dl']
        y = {}
        for pp in pairs:
            ub, vv = st['x'][pp].astype(BF16), tl(v, pp)
            y[pp] = st['z'][pp][c:] + _dot(st['g'][pp][c:].astype(BF16), cat0(lo(ub), hi(ub), lo(vv), hi(vv)))
            upd = _dot_tn(cat0(ub, vv), cat0(tl(bl, pp), tl(kl, pp)))
            rs_ref[b0 + pp[0], pp[1]] = tl(dl, pp)[0:1] * st['s_old'][pp] + jnp.where(mk.bdmask, upd, 0.0)
        ctx['y'] = _assemble(y, nb, n_pairs)

    n_steps = 0
    n = 1
    while 2 * n < c:
        n_steps += 1
        n *= 2
    return [state_read, gram, rhs0] + [solve_step] * n_steps + [solve_last, outputs]


def _mixer_kernel(x_ref, xn_ref, mn_ref, win_ref, lbl_ref, hon_ref, mu_ref, w0_ref, a0_ref, wcomb_ref,
                  kk_ref, ka_ref, rk_ref, gnw_ref, gnb_ref, wout_ref, o_ref,
                  p_ref, hs_ref, rs_ref, carry_ref, *, nb, n_groups, w_a, w_b):
    c = CHUNK
    rows = nb * c
    d_model = x_ref.shape[-1]
    n_rkv = 3 * w_b + LORA_PAD

    def in_proj_stages(grp, load_x):
        prow = slice(grp * rows, (grp + 1) * rows)
        hn = {}

        def norm():
            x = load_x().reshape(rows, d_model)
            hn['h'] = _rms_norm(x, mn_ref[...]).astype(BF16)

        def cols(lo, hi):
            def project():
                p_ref[prow, lo:hi] = _dot(hn['h'], win_ref[:, lo:hi])
            return project

        n_cols = 4 * w_a + n_rkv
        return [norm] + [cols(lo, min(lo + IN_PROJ_COL_TILE, n_cols))
                         for lo in range(0, n_cols, IN_PROJ_COL_TILE)]

    group_x = lambda grp: (lambda: x_ref[grp * nb:(grp + 1) * nb])

    @pl.when(pl.program_id(1) == 0)
    def _():
        hs_ref[...] = jnp.zeros_like(hs_ref)
        rs_ref[...] = jnp.zeros_like(rs_ref)
        carry_ref[...] = jnp.zeros_like(carry_ref)
        _merge_stages(in_proj_stages(0, group_x(0)))

    mk = _Masks(c)
    logits = lbl_ref[...]
    e = jnp.exp(logits - jnp.max(logits, axis=0, keepdims=True))
    lb = e[0:1] / jnp.sum(e, axis=0, keepdims=True)

    def prep_stages(grp, ctx):
        bctxs = [dict() for _ in range(nb)]
        per_batch = [prep_batch_stages(grp * nb + b, bctxs[b]) for b in range(nb)]

        def collect():
            rows_of = lambda name: jnp.concatenate([bc_[name] for bc_ in bctxs], axis=0)
            for name in ('hg_qt', 'hg_kt', 'hg_kl', 'hg_q0', 'hg_d', 'hg_v', 'hg_gate', 'bonus', 'gate', 'rw_dl'):
                ctx[name] = rows_of(name)
            ctx['rw'] = tuple(jnp.concatenate([bc_['rw'][i] for bc_ in bctxs], axis=0) for i in range(9))

        return [fn for stage in zip(*per_batch) for fn in stage] + [collect]

    def prep_batch_stages(bidx, ctx):
        prow = slice(bidx * c, (bidx + 1) * c)

        def hgrn_gates():
            q_a = p_ref[prow, 0:w_a]
            f_a = p_ref[prow, w_a:2 * w_a]
            forget = lb + (1.0 - lb) * _sigmoid(f_a)
            ctx['hg_logf'] = _split_bf16(jnp.log(forget))
            ctx['hg_kh'] = 1.0 - forget
            ctx['hg_q'] = q_a * _sigmoid(q_a)
            g_a = p_ref[prow, 3 * w_a:4 * w_a]
            ctx['hg_gate'] = hon_ref[...] * (g_a * _sigmoid(g_a))

        def hgrn_decays():
            hi, lo = ctx['hg_logf']
            bc = _dot(mk.tri2, jnp.concatenate([hi, lo], axis=0))
            q, kh = ctx['hg_q'], ctx['hg_kh']
            b_ref = _block_rows(bc, c // 2)
            qt = q * jnp.exp(bc - b_ref)
            kt = kh * jnp.exp(b_ref - bc)
            ctx['hg_qt'] = qt.astype(BF16)
            ctx['hg_kt'] = kt.astype(BF16)
            ctx['hg_kl'] = (kt * _block_rows(bc, c - 1, lambda last, ref: jnp.exp(last - ref))).astype(BF16)
            ctx['hg_q0'] = (qt * _block_rows(bc, c - 1, lambda last, ref: jnp.exp(ref))).astype(BF16)
            ctx['hg_d'] = _block_rows(bc, c - 1, lambda last, ref: jnp.exp(last))
            ctx['hg_v'] = p_ref[prow, 2 * w_a:3 * w_a].astype(BF16)

        def rwkv_shift():
            pb = p_ref[prow, 4 * w_a:4 * w_a + n_rkv]
            prev = jnp.broadcast_to(carry_ref[bidx:bidx + 1, :], (c, n_rkv))
            shifted = jnp.where(mk.row0, prev, pltpu.roll(pb, 1, 0))
            carry_ref[bidx:bidx + 1, :] = pb[c - 1:c, :]
            pb = pb + mu_ref[...] * (shifted - pb)
            ctx['r'] = pb[:, 0:w_b]
            ctx['k'] = pb[:, w_b:2 * w_b]
            ctx['v'] = pb[:, 2 * w_b:3 * w_b]
            low = pb[:, 3 * w_b:]
            ll = mk.lora_lane
            act = jnp.where(ll < DECAY_LORA, jnp.tanh(low),
                            jnp.where(ll < DECAY_LORA + AAA_LORA, low,
                                      jnp.where(ll < DECAY_LORA + AAA_LORA + GATE_LORA, _sigmoid(low), 0.0)))
            ctx['act'] = act.astype(BF16)

        def rwkv_lora():
            r, k = ctx['r'], ctx['k']
            ld = _dot(ctx['act'], wcomb_ref[...])
            lw = -DECAY_SCALE * _sigmoid(w0_ref[...] + ld[:, 0:w_b])
            a = _sigmoid(a0_ref[...] + ld[:, w_b:2 * w_b])
            ctx['gate'] = ld[:, 2 * w_b:3 * w_b]
            kk = k * kk_ref[...]
            k = k * (1.0 + (a - 1.0) * ka_ref[...])
            ctx['k'], ctx['a'], ctx['kk'], ctx['lw'] = k, a, kk, lw
            ctx['lw_split'] = _split_bf16(lw)
            ctx['kk_sq'] = kk * kk
            ctx['rk'] = r * k * rk_ref[...]

        def rwkv_sums():
            hi, lo = ctx['lw_split']
            ctx['gc'] = _dot(mk.tri2, jnp.concatenate([hi, lo], axis=0))
            kk = ctx['kk'] * lax.rsqrt(jnp.maximum(_segsum(ctx['kk_sq'], mk.lane_lo), L2_EPS * L2_EPS))
            ctx['bonus'] = _segsum(ctx['rk'], mk.lane_lo) * ctx['v']
            ctx['av'], ctx['bv'] = -kk, kk * ctx['a']

        def rwkv_decays():
            r, k, v, av, bv, lw, gc = (ctx[n] for n in ('r', 'k', 'v', 'av', 'bv', 'lw', 'gc'))
            gref = _block_rows(gc, c // 2)
            e_up = jnp.exp(gc - gref)
            e_dn = jnp.exp(gref - gc)
            at, rt, bt, kt = av * jnp.exp(gc - lw - gref), r * e_up, bv * e_dn, k * e_dn
            s_ref = _block_rows(gc, c - 1, lambda last, ref: jnp.exp(ref))
            s_last = _block_rows(gc, c - 1, lambda last, ref: jnp.exp(last - ref))
            full = (at, rt, bt, kt, at * s_ref, rt * s_ref, bt * s_last, kt * s_last, v)
            ctx['rw'] = tuple(t.astype(BF16) for t in full)
            ctx['rw_dl'] = _block_rows(gc, c - 1, lambda last, ref: jnp.exp(last))

        return [hgrn_gates, hgrn_decays, rwkv_shift, rwkv_lora, rwkv_sums, rwkv_decays]

    def chain_stages(grp, ctx):
        b0 = grp * nb
        return _hgrn_stages(ctx, hs_ref, b0, nb, w_a, mk) + _rwkv_stages(ctx, rs_ref, b0, nb, w_b, mk)

    def post_stages(grp, ctx):
        b0 = grp * nb
        inv_n = 1.0 / HB_HEAD_DIM
        parts = [dict() for _ in range(nb)]

        def batch_stages(b, st):
            brow = slice(b * c, (b + 1) * c)

            def gate_a():
                st['o_a'] = (ctx['o_a'][brow] * ctx['hg_gate'][brow]).astype(BF16)
                st['y'] = ctx['y'][brow]

            def gn_mean():
                yc = st['y'] - _segsum(st['y'], mk.lane_lo) * inv_n
                st['yc'], st['yc_sq'] = yc, yc * yc

            def gn_var():
                var = _segsum(st['yc_sq'], mk.lane_lo) * inv_n
                yn = st['yc'] * lax.rsqrt(var + RWKV_GN_EPS) * gnw_ref[...] + gnb_ref[...]
                o_b = ((yn + ctx['bonus'][brow]) * ctx['gate'][brow]).astype(BF16)
                st['o'] = jnp.concatenate([st['o_a'], o_b], axis=1)

            return [gate_a, gn_mean, gn_var]

        def out_proj():
            o = jnp.concatenate([st['o'] for st in parts], axis=0)
            x = x_ref[b0:b0 + nb].reshape(rows, d_model)
            o_ref[b0:b0 + nb] = (x + _dot(o, wout_ref[...])).reshape(nb, c, d_model)

        per_batch = [batch_stages(b, parts[b]) for b in range(nb)]
        return [fn for stage in zip(*per_batch) for fn in stage] + [out_proj]

    ctxs = [dict() for _ in range(n_groups)]
    lists = [prep_stages(0, ctxs[0])]
    if n_groups > 1:
        lists.append(in_proj_stages(1, group_x(1)))
    _merge_stages(*lists)
    next_proj = in_proj_stages(0, lambda: xn_ref[...])
    n_early = 1 + (len(next_proj) - 1) // 2
    for grp in range(n_groups):
        lists = [chain_stages(grp, ctxs[grp])]
        if grp + 1 < n_groups:
            lists.append(prep_stages(grp + 1, ctxs[grp + 1]))
        if grp + 2 < n_groups:
            lists.append(in_proj_stages(grp + 2, group_x(grp + 2)))
        if grp >= 1:
            lists.append(post_stages(grp - 1, ctxs[grp - 1]))
        if grp == 0:
            lists.append(next_proj[:n_early])
        _merge_stages(*lists)
    _merge_stages(post_stages(n_groups - 1, ctxs[n_groups - 1]), next_proj[n_early:])


def _mixer(x, mix_norm, w_in, lb_logits, hgrn_out_norm, mu, w0, w2, a0, a2, g2, k_k, k_a, r_k,
           gn_w, gn_b, w_out):
    bsz, t, d = x.shape
    w_a = hgrn_out_norm.shape[-1]
    w_b = w0.shape[-1]
    nb, n_groups, c = MIXER_GROUP, MIXER_GROUPS_PER_STEP, CHUNK
    nbs = nb * n_groups
    n_lora = DECAY_LORA + AAA_LORA + GATE_LORA
    assert bsz % nbs == 0 and t % c == 0
    assert w_a % HA_HEAD_DIM == 0 and w_b % LANES == 0
    assert w_in.shape[1] == 4 * w_a + 3 * w_b + n_lora
    pad = LORA_PAD - n_lora
    n_cols = 4 * w_a + 3 * w_b + LORA_PAD
    n_rkv = 3 * w_b + LORA_PAD
    w_in_p = jnp.pad(w_in, ((0, 0), (0, pad))).astype(BF16)
    mu_p = jnp.pad(mu.reshape(1, -1), ((0, 0), (0, pad)))
    wcomb = jnp.zeros((LORA_PAD, 3 * w_b), F32)
    wcomb = wcomb.at[0:DECAY_LORA, 0:w_b].set(w2)
    wcomb = wcomb.at[DECAY_LORA:DECAY_LORA + AAA_LORA, w_b:2 * w_b].set(a2)
    wcomb = wcomb.at[DECAY_LORA + AAA_LORA:n_lora, 2 * w_b:3 * w_b].set(g2)
    row = lambda vec: vec.reshape(1, -1).astype(F32)
    n_pairs = w_b // LANES
    n_heads_a = w_a // HA_HEAD_DIM
    state_bytes = nbs * (n_heads_a * HA_HEAD_DIM * HA_HEAD_DIM + n_pairs * LANES * LANES) * 4
    weight_bytes = (d * n_cols + LORA_PAD * 3 * w_b + (w_a + w_b) * d) * 2
    vmem = weight_bytes + state_bytes + 3 * nbs * c * n_cols * 4 + 4 * nbs * c * d * 4 + 12 * MIB
    return pl.pallas_call(
        functools.partial(_mixer_kernel, nb=nb, n_groups=n_groups, w_a=w_a, w_b=w_b),
        grid=(bsz // nbs, t // c),
        in_specs=[
            pl.BlockSpec((nbs, c, d), lambda i, j: (i, j, 0)),
            pl.BlockSpec((nb, c, d), lambda i, j: (i * n_groups, jnp.minimum(j + 1, t // c - 1), 0)),
            _resident((1, d)),
            _resident((d, n_cols)),
            _resident(lb_logits.shape),
            _resident((1, w_a)),
            _resident((1, n_rkv)),
            _resident((1, w_b)),
            _resident((1, w_b)),
            _resident((LORA_PAD, 3 * w_b)),
            _resident((1, w_b)),
            _resident((1, w_b)),
            _resident((1, w_b)),
            _resident((1, w_b)),
            _resident((1, w_b)),
            _resident((w_a + w_b, d)),
        ],
        out_specs=pl.BlockSpec((nbs, c, d), lambda i, j: (i, j, 0)),
        out_shape=jax.ShapeDtypeStruct((bsz, t, d), F32),
        scratch_shapes=[
            pltpu.VMEM((nbs * c, n_cols), F32),
            pltpu.VMEM((nbs, n_heads_a, HA_HEAD_DIM, HA_HEAD_DIM), F32),
            pltpu.VMEM((nbs, n_pairs, LANES, LANES), F32),
            pltpu.VMEM((nbs, n_rkv), F32),
        ],
        compiler_params=pltpu.CompilerParams(
            dimension_semantics=("arbitrary", "arbitrary"), vmem_limit_bytes=vmem),
        name="mixer",
    )(x, x, row(mix_norm), w_in_p, lb_logits.astype(F32), row(hgrn_out_norm), mu_p, row(w0), row(a0),
      wcomb.astype(BF16), row(k_k), row(k_a), row(r_k), row(gn_w), row(gn_b), w_out.astype(BF16))


def kernel(x, ffn1_norm, ffn1_w_gate, ffn1_w_up, ffn1_w_down, mix_norm, w_in, hgrn_lb_logits, hgrn_out_norm, rwkv_shift_mu, rwkv_w0, rwkv_w2, rwkv_a0, rwkv_a2, rwkv_g2, rwkv_k_k, rwkv_k_a, rwkv_r_k, rwkv_gn_w, rwkv_gn_b, w_out, ffn2_norm, ffn2_w_gate, ffn2_w_up, ffn2_w_down, final_norm):
    bsz, t, d = x.shape
    depth = ffn1_norm.shape[0]
    assert depth == 1 and hgrn_lb_logits.shape[0] == depth + 1
    l = 0
    row = lambda vec: vec.reshape(1, -1).astype(F32)
    fg = row(final_norm)
    h = _ffn(x.reshape(bsz * t, d), row(ffn1_norm[l]), ffn1_w_gate[l], ffn1_w_up[l], ffn1_w_down[l],
             fg, final_norm=False)
    h = _mixer(h.reshape(bsz, t, d), mix_norm[l], w_in[l], hgrn_lb_logits, hgrn_out_norm[l],
               rwkv_shift_mu[l], rwkv_w0[l], rwkv_w2[l], rwkv_a0[l], rwkv_a2[l], rwkv_g2[l],
               rwkv_k_k[l], rwkv_k_a[l], rwkv_r_k[l], rwkv_gn_w[l], rwkv_gn_b[l], w_out[l])
    h = _ffn(h.reshape(bsz * t, d), row(ffn2_norm[l]), ffn2_w_gate[l], ffn2_w_up[l], ffn2_w_down[l],
             fg, final_norm=True)
    return h.reshape(bsz, t, d)
```

```python
import functools
import math

import jax
import jax.numpy as jnp
from jax import lax
from jax.experimental import pallas as pl
from jax.experimental.pallas import tpu as pltpu

F32 = jnp.float32
BF16 = jnp.bfloat16

NORM_EPS = 1e-6
RWKV_GN_EPS = 64e-5
L2_EPS = 1e-12
DECAY_SCALE = math.exp(-0.5)

LANES = 128
MXU_COLS = 256
HA_HEAD_DIM = 128
HB_HEAD_DIM = 64
CHUNK = 64
DECAY_LORA, AAA_LORA, GATE_LORA = 32, 32, 96
LORA_PAD = 256
FFN_ROW_TILE = 1024
FFN_COL_TILE = 768
FFN_WEIGHT_SLABS = 8
MIXER_WEIGHT_SLABS = 8
MIXER_GROUP = 4
MIXER_GROUPS_PER_STEP = 2
IN_PROJ_COL_TILE = 512
MIB = 1024 * 1024


def _dot(a, b):
    return jnp.dot(a, b, preferred_element_type=F32)


def _dot_nt(a, b):
    return lax.dot_general(a, b, (((1,), (1,)), ((), ())), preferred_element_type=F32)


def _dot_tn(a, b):
    return lax.dot_general(a, b, (((0,), (0,)), ((), ())), preferred_element_type=F32)


def _split_bf16(x):
    hi = x.astype(BF16)
    lo = (x - hi.astype(F32)).astype(BF16)
    return hi, lo


def _rms_norm(x, g):
    return x * lax.rsqrt(jnp.mean(x * x, axis=-1, keepdims=True) + NORM_EPS) * g


def _sigmoid(x):
    return 0.5 * jnp.tanh(0.5 * x) + 0.5


def _ffn_kernel(x_ref, g_ref, wg_ref, wu_ref, wd_ref, fn_ref, o_ref, wg_s, wu_s, wd_s, *,
                col_tiles, final_norm, n_load):
    i = pl.program_id(0)

    @pl.when(i < n_load)
    def _():
        rg, rd = wg_ref.shape[0], wd_ref.shape[0]
        wg_s[pl.ds(pl.multiple_of(i * rg, rg), rg), :] = wg_ref[...].astype(BF16)
        wu_s[pl.ds(pl.multiple_of(i * rg, rg), rg), :] = wu_ref[...].astype(BF16)
        wd_s[pl.ds(pl.multiple_of(i * rd, rd), rd), :] = wd_ref[...].astype(BF16)

    @pl.when(i >= n_load)
    def _():
        x = x_ref[...]
        h = _rms_norm(x, g_ref[...]).astype(BF16)
        acc = jnp.zeros(x.shape, F32)
        for lo, hi in col_tiles:
            gate = _dot(h, wg_s[:, lo:hi])
            up = _dot(h, wu_s[:, lo:hi])
            act = (gate * _sigmoid(gate) * up).astype(BF16)
            acc = acc + _dot(act, wd_s[lo:hi, :])
        y = x + 0.5 * acc
        if final_norm:
            y = _rms_norm(y, fn_ref[...])
        o_ref[...] = y


def _resident(shape):
    return pl.BlockSpec(shape, lambda *_: (0,) * len(shape), pipeline_mode=pl.Buffered(1))


def _ffn(x2d, norm_g, w_gate, w_up, w_down, final_g, *, final_norm):
    n, d = x2d.shape
    d_ff = w_gate.shape[1]
    tm, n_load = FFN_ROW_TILE, FFN_WEIGHT_SLABS
    assert n % tm == 0 and d_ff % MXU_COLS == 0
    assert d % (16 * n_load) == 0 and d_ff % (16 * n_load) == 0
    col_tiles = tuple((lo, min(lo + FFN_COL_TILE, d_ff)) for lo in range(0, d_ff, FFN_COL_TILE))
    weight_bytes = 3 * d * d_ff * 2
    slab_bytes = 3 * d * d_ff * 4 // n_load
    tile_bytes = tm * d * 4
    vmem = weight_bytes + 2 * slab_bytes + 8 * tile_bytes + 3 * tm * FFN_COL_TILE * 4 + 8 * MIB
    slab = lambda i: (jnp.minimum(i, n_load - 1), 0)
    tile = lambda i: (jnp.maximum(i - n_load, 0), 0)
    return pl.pallas_call(
        functools.partial(_ffn_kernel, col_tiles=col_tiles, final_norm=final_norm, n_load=n_load),
        grid=(n_load + n // tm,),
        in_specs=[
            pl.BlockSpec((tm, d), tile),
            _resident((1, d)),
            pl.BlockSpec((d // n_load, d_ff), slab),
            pl.BlockSpec((d // n_load, d_ff), slab),
            pl.BlockSpec((d_ff // n_load, d), slab),
            _resident((1, d)),
        ],
        out_specs=pl.BlockSpec((tm, d), tile),
        out_shape=jax.ShapeDtypeStruct((n, d), F32),
        scratch_shapes=[
            pltpu.VMEM((d, d_ff), BF16),
            pltpu.VMEM((d, d_ff), BF16),
            pltpu.VMEM((d_ff, d), BF16),
        ],
        compiler_params=pltpu.CompilerParams(
            dimension_semantics=("arbitrary",), vmem_limit_bytes=vmem),
        name="ffn_final" if final_norm else "ffn",
    )(x2d, norm_g, w_gate, w_up, w_down, final_g)


def _iota2(shape, axis):
    return lax.broadcasted_iota(jnp.int32, shape, axis)


def _segsum(x, lane_lo):
    outs = []
    for j in range(x.shape[1] // LANES):
        t = x[:, j * LANES:(j + 1) * LANES]
        lo = jnp.sum(jnp.where(lane_lo, t, 0.0), axis=-1, keepdims=True)
        hi = jnp.sum(jnp.where(lane_lo, 0.0, t), axis=-1, keepdims=True)
        outs.append(jnp.where(lane_lo, lo, hi))
    return jnp.concatenate(outs, axis=1)


def _block_rows(x, idx, fn=None):
    c = CHUNK
    outs = []
    for r in range(0, x.shape[0], c):
        row = x[r + idx:r + idx + 1]
        if fn is not None:
            row = fn(row, x[r + c // 2:r + c // 2 + 1])
        outs.append(jnp.broadcast_to(row, (c, x.shape[1])))
    return jnp.concatenate(outs, axis=0)


def _tile(x, b, j, width):
    return x[b * CHUNK:(b + 1) * CHUNK, j * width:(j + 1) * width]


def _assemble(tiles, nb, nj):
    return jnp.concatenate(
        [jnp.concatenate([tiles[(b, j)] for j in range(nj)], axis=1) for b in range(nb)], axis=0)


def _merge_stages(*stage_lists):
    tagged = []
    for stages in stage_lists:
        tagged += [((i + 0.5) / len(stages), i, fn) for i, fn in enumerate(stages)]
    for _, _, fn in sorted(tagged, key=lambda t: t[:2]):
        fn()


class _Masks:
    def __init__(self, rows):
        c = CHUNK
        ri, ci = _iota2((c, c), 0), _iota2((c, c), 1)
        self.causal = ci <= ri
        bi, bj = _iota2((rows, rows), 0), _iota2((rows, rows), 1)
        same_block = lax.shift_right_logical(bi, 6) == lax.shift_right_logical(bj, 6)
        tri = ((bj <= bi) & same_block).astype(BF16)
        self.tri2 = jnp.concatenate([tri, tri], axis=1)
        gi, gj = _iota2((2 * c, 4 * c), 0), _iota2((2 * c, 4 * c), 1)
        ti, sj = gi & (c - 1), gj & (c - 1)
        self.gmask = (sj < ti) | ((gi >= c) & (sj == ti))
        li, lj = _iota2((LANES, LANES), 0), _iota2((LANES, LANES), 1)
        self.bdmask = lax.shift_right_logical(li, 6) == lax.shift_right_logical(lj, 6)
        self.lane_lo = _iota2((1, LANES), 1) < HB_HEAD_DIM
        self.row0 = (_iota2((rows, 1), 0) & (c - 1)) == 0
        self.lora_lane = _iota2((1, LORA_PAD), 1)


def _hgrn_stages(ctx, hs_ref, b0, nb, w_a, mk):
    hd = HA_HEAD_DIM
    chains = [(b, h) for b in range(nb) for h in range(w_a // hd)]
    st = {}

    def scores():
        qt, kt = ctx['hg_qt'], ctx['hg_kt']
        st['s_old'] = {ch: hs_ref[b0 + ch[0], ch[1]] for ch in chains}
        st['sc'] = {ch: jnp.where(mk.causal, _dot_nt(_tile(qt, *ch, hd), _tile(kt, *ch, hd)), 0.0).astype(BF16)
                    for ch in chains}

    def outputs():
        v, q0, kl, d = ctx['hg_v'], ctx['hg_q0'], ctx['hg_kl'], ctx['hg_d']
        s_old = st['s_old']
        o = {ch: _dot(st['sc'][ch], _tile(v, *ch, hd)) + _dot_nt(_tile(q0, *ch, hd), s_old[ch].astype(BF16))
             for ch in chains}
        for ch in chains:
            hs_ref[b0 + ch[0], ch[1]] = (_tile(d, *ch, hd)[0:1] * s_old[ch]
                                         + _dot_tn(_tile(v, *ch, hd), _tile(kl, *ch, hd)))
        st['o'] = o

    def normalise():
        o = {ch: t * lax.rsqrt(jnp.mean(t * t, axis=-1, keepdims=True) + NORM_EPS) for ch, t in st['o'].items()}
        ctx['o_a'] = _assemble(o, nb, w_a // hd)

    return [scores, outputs, normalise]


def _rwkv_stages(ctx, rs_ref, b0, nb, w_b, mk):
    c = CHUNK
    n_pairs = w_b // LANES
    lane_lo = mk.lane_lo
    pairs = [(b, pr) for b in range(nb) for pr in range(n_pairs)]
    tl = lambda t, pair: _tile(t, pair[0], pair[1], LANES)
    zero = jnp.zeros((c, LANES), BF16)
    lo = lambda t: jnp.where(lane_lo, t, zero)
    hi = lambda t: jnp.where(lane_lo, zero, t)
    cat0 = lambda *ts: jnp.concatenate(ts, axis=0)
    st = {}

    def state_read():
        at, rt, bt, kt, a0, r0, bl, kl, v = ctx['rw']
        st['s_old'] = {pp: rs_ref[b0 + pp[0], pp[1]] for pp in pairs}
        st['z'] = {pp: _dot_nt(cat0(tl(a0, pp), tl(r0, pp)), st['s_old'][pp].astype(BF16)) for pp in pairs}

    def gram():
        at, rt, bt, kt, a0, r0, bl, kl, v = ctx['rw']
        g = {}
        for pp in pairs:
            b_, k_ = tl(bt, pp), tl(kt, pp)
            g[pp] = jnp.where(
                mk.gmask, _dot_nt(cat0(tl(at, pp), tl(rt, pp)), cat0(lo(b_), hi(b_), lo(k_), hi(k_))), 0.0)
        st['g'] = g

    def rhs0():
        v = ctx['rw'][8]
        x, pw = {}, {}
        for pp in pairs:
            pw[pp] = st['g'][pp][:c, :LANES].astype(BF16)
            lak = st['g'][pp][:c, LANES:].astype(BF16)
            x[pp] = st['z'][pp][:c] + _dot(lak, cat0(lo(tl(v, pp)), hi(tl(v, pp))))
        st['x'], st['pw'] = x, pw

    def solve_step():
        x, pw, res = st['x'], st['pw'], {}
        for pp in pairs:
            xb = x[pp].astype(BF16)
            rhs = jnp.concatenate([cat0(lo(xb), hi(xb)), cat0(lo(pw[pp]), hi(pw[pp]))], axis=1)
            res[pp] = _dot(pw[pp], rhs)
        st['x'] = {pp: x[pp] + res[pp][:, :LANES] for pp in pairs}
        st['pw'] = {pp: res[pp][:, LANES:].astype(BF16) for pp in pairs}

    def solve_last():
        x, pw = st['x'], st['pw']
        for pp in pairs:
            xb = x[pp].astype(BF16)
            x[pp] = x[pp] + _dot(pw[pp], cat0(lo(xb), hi(xb)))

    def outputs():
        at, rt, bt, kt, a0, r0, bl, kl, v = ctx['rw']
        dl = ctx['rw_dl']
        y = {}
        for pp in pairs:
            ub, vv = st['x'][pp].astype(BF16), tl(v, pp)
            y[pp] = st['z'][pp][c:] + _dot(st['g'][pp][c:].astype(BF16), cat0(lo(ub), hi(ub), lo(vv), hi(vv)))
            upd = _dot_tn(cat0(ub, vv), cat0(tl(bl, pp), tl(kl, pp)))
            rs_ref[b0 + pp[0], pp[1]] = tl(dl, pp)[0:1] * st['s_old'][pp] + jnp.where(mk.bdmask, upd, 0.0)
        ctx['y'] = _assemble(y, nb, n_pairs)

    n_steps = 0
    n = 1
    while 2 * n < c:
        n_steps += 1
        n *= 2
    return [state_read, gram, rhs0] + [solve_step] * n_steps + [solve_last, outputs]


def _mixer_kernel(x_ref, xn_ref, mn_ref, win_f32, lbl_ref, hon_ref, mu_ref, w0_ref, a0_ref, wcomb_ref,
                  kk_ref, ka_ref, rk_ref, gnw_ref, gnb_ref, wout_f32, o_ref,
                  p_ref, hs_ref, rs_ref, carry_ref, win_s, wout_s, *, n_load, **dims):
    j = pl.program_id(1)

    @pl.when(j < n_load)
    def _():
        ri, ro = win_f32.shape[0], wout_f32.shape[0]
        n_in = win_f32.shape[1]
        rows_in = pl.ds(pl.multiple_of(j * ri, ri), ri)
        win_s[rows_in, 0:n_in] = win_f32[...].astype(BF16)
        win_s[rows_in, n_in:] = jnp.zeros((ri, win_s.shape[1] - n_in), BF16)
        wout_s[pl.ds(pl.multiple_of(j * ro, ro), ro), :] = wout_f32[...].astype(BF16)

    @pl.when(j >= n_load)
    def _():
        _mixer_step(x_ref, xn_ref, mn_ref, win_s, lbl_ref, hon_ref, mu_ref, w0_ref, a0_ref, wcomb_ref,
                    kk_ref, ka_ref, rk_ref, gnw_ref, gnb_ref, wout_s, o_ref,
                    p_ref, hs_ref, rs_ref, carry_ref, first_step=n_load, **dims)


def _mixer_step(x_ref, xn_ref, mn_ref, win_ref, lbl_ref, hon_ref, mu_ref, w0_ref, a0_ref, wcomb_ref,
                kk_ref, ka_ref, rk_ref, gnw_ref, gnb_ref, wout_ref, o_ref,
                p_ref, hs_ref, rs_ref, carry_ref, *, first_step, nb, n_groups, w_a, w_b):
    c = CHUNK
    rows = nb * c
    d_model = x_ref.shape[-1]
    n_rkv = 3 * w_b + LORA_PAD

    def in_proj_stages(grp, load_x):
        prow = slice(grp * rows, (grp + 1) * rows)
        hn = {}

        def norm():
            x = load_x().reshape(rows, d_model)
            hn['h'] = _rms_norm(x, mn_ref[...]).astype(BF16)

        def cols(lo, hi):
            def project():
                p_ref[prow, lo:hi] = _dot(hn['h'], win_ref[:, lo:hi])
            return project

        n_cols = 4 * w_a + n_rkv
        return [norm] + [cols(lo, min(lo + IN_PROJ_COL_TILE, n_cols))
                         for lo in range(0, n_cols, IN_PROJ_COL_TILE)]

    group_x = lambda grp: (lambda: x_ref[grp * nb:(grp + 1) * nb])

    @pl.when(pl.program_id(1) == first_step)
    def _():
        hs_ref[...] = jnp.zeros_like(hs_ref)
        rs_ref[...] = jnp.zeros_like(rs_ref)
        carry_ref[...] = jnp.zeros_like(carry_ref)
        _merge_stages(in_proj_stages(0, group_x(0)))

    mk = _Masks(c)
    logits = lbl_ref[...]
    e = jnp.exp(logits - jnp.max(logits, axis=0, keepdims=True))
    lb = e[0:1] / jnp.sum(e, axis=0, keepdims=True)

    def prep_stages(grp, ctx):
        bctxs = [dict() for _ in range(nb)]
        per_batch = [prep_batch_stages(grp * nb + b, bctxs[b]) for b in range(nb)]

        def collect():
            rows_of = lambda name: jnp.concatenate([bc_[name] for bc_ in bctxs], axis=0)
            for name in ('hg_qt', 'hg_kt', 'hg_kl', 'hg_q0', 'hg_d', 'hg_v', 'hg_gate', 'bonus', 'gate', 'rw_dl'):
                ctx[name] = rows_of(name)
            ctx['rw'] = tuple(jnp.concatenate([bc_['rw'][i] for bc_ in bctxs], axis=0) for i in range(9))

        return [fn for stage in zip(*per_batch) for fn in stage] + [collect]

    def prep_batch_stages(bidx, ctx):
        prow = slice(bidx * c, (bidx + 1) * c)

        def hgrn_gates():
            q_a = p_ref[prow, 0:w_a]
            f_a = p_ref[prow, w_a:2 * w_a]
            forget = lb + (1.0 - lb) * _sigmoid(f_a)
            ctx['hg_logf'] = _split_bf16(jnp.log(forget))
            ctx['hg_kh'] = 1.0 - forget
            ctx['hg_q'] = q_a * _sigmoid(q_a)
            g_a = p_ref[prow, 3 * w_a:4 * w_a]
            ctx['hg_gate'] = hon_ref[...] * (g_a * _sigmoid(g_a))

        def hgrn_decays():
            hi, lo = ctx['hg_logf']
            bc = _dot(mk.tri2, jnp.concatenate([hi, lo], axis=0))
            q, kh = ctx['hg_q'], ctx['hg_kh']
            b_ref = _block_rows(bc, c // 2)
            qt = q * jnp.exp(bc - b_ref)
            kt = kh * jnp.exp(b_ref - bc)
            ctx['hg_qt'] = qt.astype(BF16)
            ctx['hg_kt'] = kt.astype(BF16)
            ctx['hg_kl'] = (kt * _block_rows(bc, c - 1, lambda last, ref: jnp.exp(last - ref))).astype(BF16)
            ctx['hg_q0'] = (qt * _block_rows(bc, c - 1, lambda last, ref: jnp.exp(ref))).astype(BF16)
            ctx['hg_d'] = _block_rows(bc, c - 1, lambda last, ref: jnp.exp(last))
            ctx['hg_v'] = p_ref[prow, 2 * w_a:3 * w_a].astype(BF16)

        def rwkv_shift():
            pb = p_ref[prow, 4 * w_a:4 * w_a + n_rkv]
            prev = jnp.broadcast_to(carry_ref[bidx:bidx + 1, :], (c, n_rkv))
            shifted = jnp.where(mk.row0, prev, pltpu.roll(pb, 1, 0))
            carry_ref[bidx:bidx + 1, :] = pb[c - 1:c, :]
            pb = pb + mu_ref[...] * (shifted - pb)
            ctx['r'] = pb[:, 0:w_b]
            ctx['k'] = pb[:, w_b:2 * w_b]
            ctx['v'] = pb[:, 2 * w_b:3 * w_b]
            low = pb[:, 3 * w_b:]
            ll = mk.lora_lane
            act = jnp.where(ll < DECAY_LORA, jnp.tanh(low),
                            jnp.where(ll < DECAY_LORA + AAA_LORA, low,
                                      jnp.where(ll < DECAY_LORA + AAA_LORA + GATE_LORA, _sigmoid(low), 0.0)))
            ctx['act'] = act.astype(BF16)

        def rwkv_lora():
            r, k = ctx['r'], ctx['k']
            ld = _dot(ctx['act'], wcomb_ref[...])
            lw = -DECAY_SCALE * _sigmoid(w0_ref[...] + ld[:, 0:w_b])
            a = _sigmoid(a0_ref[...] + ld[:, w_b:2 * w_b])
            ctx['gate'] = ld[:, 2 * w_b:3 * w_b]
            kk = k * kk_ref[...]
            k = k * (1.0 + (a - 1.0) * ka_ref[...])
            ctx['k'], ctx['a'], ctx['kk'], ctx['lw'] = k, a, kk, lw
            ctx['lw_split'] = _split_bf16(lw)
            ctx['kk_sq'] = kk * kk
            ctx['rk'] = r * k * rk_ref[...]

        def rwkv_sums():
            hi, lo = ctx['lw_split']
            ctx['gc'] = _dot(mk.tri2, jnp.concatenate([hi, lo], axis=0))
            kk = ctx['kk'] * lax.rsqrt(jnp.maximum(_segsum(ctx['kk_sq'], mk.lane_lo), L2_EPS * L2_EPS))
            ctx['bonus'] = _segsum(ctx['rk'], mk.lane_lo) * ctx['v']
            ctx['av'], ctx['bv'] = -kk, kk * ctx['a']

        def rwkv_decays():
            r, k, v, av, bv, lw, gc = (ctx[n] for n in ('r', 'k', 'v', 'av', 'bv', 'lw', 'gc'))
            gref = _block_rows(gc, c // 2)
            e_up = jnp.exp(gc - gref)
            e_dn = jnp.exp(gref - gc)
            at, rt, bt, kt = av * jnp.exp(gc - lw - gref), r * e_up, bv * e_dn, k * e_dn
            s_ref = _block_rows(gc, c - 1, lambda last, ref: jnp.exp(ref))
            s_last = _block_rows(gc, c - 1, lambda last, ref: jnp.exp(last - ref))
            full = (at, rt, bt, kt, at * s_ref, rt * s_ref, bt * s_last, kt * s_last, v)
            ctx['rw'] = tuple(t.astype(BF16) for t in full)
            ctx['rw_dl'] = _block_rows(gc, c - 1, lambda last, ref: jnp.exp(last))

        return [hgrn_gates, hgrn_decays, rwkv_shift, rwkv_lora, rwkv_sums, rwkv_decays]

    def chain_stages(grp, ctx):
        b0 = grp * nb
        return _hgrn_stages(ctx, hs_ref, b0, nb, w_a, mk) + _rwkv_stages(ctx, rs_ref, b0, nb, w_b, mk)

    def post_stages(grp, ctx):
        b0 = grp * nb
        inv_n = 1.0 / HB_HEAD_DIM
        parts = [dict() for _ in range(nb)]

        def batch_stages(b, st):
            brow = slice(b * c, (b + 1) * c)

            def gate_a():
                st['o_a'] = (ctx['o_a'][brow] * ctx['hg_gate'][brow]).astype(BF16)
                st['y'] = ctx['y'][brow]

            def gn_mean():
                yc = st['y'] - _segsum(st['y'], mk.lane_lo) * inv_n
                st['yc'], st['yc_sq'] = yc, yc * yc

            def gn_var():
                var = _segsum(st['yc_sq'], mk.lane_lo) * inv_n
                yn = st['yc'] * lax.rsqrt(var + RWKV_GN_EPS) * gnw_ref[...] + gnb_ref[...]
                o_b = ((yn + ctx['bonus'][brow]) * ctx['gate'][brow]).astype(BF16)
                st['o'] = jnp.concatenate([st['o_a'], o_b], axis=1)

            return [gate_a, gn_mean, gn_var]

        def out_proj():
            o = jnp.concatenate([st['o'] for st in parts], axis=0)
            x = x_ref[b0:b0 + nb].reshape(rows, d_model)
            o_ref[b0:b0 + nb] = (x + _dot(o, wout_ref[...])).reshape(nb, c, d_model)

        per_batch = [batch_stages(b, parts[b]) for b in range(nb)]
        return [fn for stage in zip(*per_batch) for fn in stage] + [out_proj]

    ctxs = [dict() for _ in range(n_groups)]
    lists = [prep_stages(0, ctxs[0])]
    if n_groups > 1:
        lists.append(in_proj_stages(1, group_x(1)))
    _merge_stages(*lists)
    next_proj = in_proj_stages(0, lambda: xn_ref[...])
    n_early = 1 + (len(next_proj) - 1) // 2
    for grp in range(n_groups):
        lists = [chain_stages(grp, ctxs[grp])]
        if grp + 1 < n_groups:
            lists.append(prep_stages(grp + 1, ctxs[grp + 1]))
        if grp + 2 < n_groups:
            lists.append(in_proj_stages(grp + 2, group_x(grp + 2)))
        if grp >= 1:
            lists.append(post_stages(grp - 1, ctxs[grp - 1]))
        if grp == 0:
            lists.append(next_proj[:n_early])
        _merge_stages(*lists)
    _merge_stages(post_stages(n_groups - 1, ctxs[n_groups - 1]), next_proj[n_early:])


def _mixer(x, mix_norm, w_in, lb_logits, hgrn_out_norm, mu, w0, w2, a0, a2, g2, k_k, k_a, r_k,
           gn_w, gn_b, w_out):
    bsz, t, d = x.shape
    w_a = hgrn_out_norm.shape[-1]
    w_b = w0.shape[-1]
    nb, n_groups, c = MIXER_GROUP, MIXER_GROUPS_PER_STEP, CHUNK
    nbs = nb * n_groups
    n_lora = DECAY_LORA + AAA_LORA + GATE_LORA
    assert bsz % nbs == 0 and t % c == 0
    assert w_a % HA_HEAD_DIM == 0 and w_b % LANES == 0
    assert w_in.shape[1] == 4 * w_a + 3 * w_b + n_lora
    pad = LORA_PAD - n_lora
    n_cols = 4 * w_a + 3 * w_b + LORA_PAD
    n_rkv = 3 * w_b + LORA_PAD
    mu_p = jnp.pad(mu.reshape(1, -1), ((0, 0), (0, pad)))
    wcomb = jnp.zeros((LORA_PAD, 3 * w_b), F32)
    wcomb = wcomb.at[0:DECAY_LORA, 0:w_b].set(w2)
    wcomb = wcomb.at[DECAY_LORA:DECAY_LORA + AAA_LORA, w_b:2 * w_b].set(a2)
    wcomb = wcomb.at[DECAY_LORA + AAA_LORA:n_lora, 2 * w_b:3 * w_b].set(g2)
    row = lambda vec: vec.reshape(1, -1).astype(F32)
    n_pairs = w_b // LANES
    n_heads_a = w_a // HA_HEAD_DIM
    state_bytes = nbs * (n_heads_a * HA_HEAD_DIM * HA_HEAD_DIM + n_pairs * LANES * LANES) * 4
    weight_bytes = (d * n_cols + LORA_PAD * 3 * w_b + (w_a + w_b) * d) * 2
    n_load, nt = MIXER_WEIGHT_SLABS, t // c
    assert bsz == nbs and d % (16 * n_load) == 0 and (w_a + w_b) % (16 * n_load) == 0
    slab_bytes = (d * w_in.shape[1] + (w_a + w_b) * d) * 4 // n_load
    vmem = (weight_bytes + 2 * slab_bytes + state_bytes + 3 * nbs * c * n_cols * 4 + 4 * nbs * c * d * 4
            + 12 * MIB)
    slab = lambda i, j: (jnp.minimum(j, n_load - 1), 0)
    chunk = lambda i, j: (i, jnp.maximum(j - n_load, 0), 0)
    return pl.pallas_call(
        functools.partial(_mixer_kernel, n_load=n_load, nb=nb, n_groups=n_groups, w_a=w_a, w_b=w_b),
        grid=(1, n_load + nt),
        in_specs=[
            pl.BlockSpec((nbs, c, d), chunk),
            pl.BlockSpec((nb, c, d), lambda i, j: (0, jnp.clip(j - n_load + 1, 0, nt - 1), 0)),
            _resident((1, d)),
            pl.BlockSpec((d // n_load, w_in.shape[1]), slab),
            _resident(lb_logits.shape),
            _resident((1, w_a)),
            _resident((1, n_rkv)),
            _resident((1, w_b)),
            _resident((1, w_b)),
            _resident((LORA_PAD, 3 * w_b)),
            _resident((1, w_b)),
            _resident((1, w_b)),
            _resident((1, w_b)),
            _resident((1, w_b)),
            _resident((1, w_b)),
            pl.BlockSpec(((w_a + w_b) // n_load, d), slab),
        ],
        out_specs=pl.BlockSpec((nbs, c, d), chunk),
        out_shape=jax.ShapeDtypeStruct((bsz, t, d), F32),
        scratch_shapes=[
            pltpu.VMEM((nbs * c, n_cols), F32),
            pltpu.VMEM((nbs, n_heads_a, HA_HEAD_DIM, HA_HEAD_DIM), F32),
            pltpu.VMEM((nbs, n_pairs, LANES, LANES), F32),
            pltpu.VMEM((nbs, n_rkv), F32),
            pltpu.VMEM((d, n_cols), BF16),
            pltpu.VMEM((w_a + w_b, d), BF16),
        ],
        compiler_params=pltpu.CompilerParams(
            dimension_semantics=("arbitrary", "arbitrary"), vmem_limit_bytes=vmem),
        name="mixer",
    )(x, x, row(mix_norm), w_in, lb_logits.astype(F32), row(hgrn_out_norm), mu_p, row(w0), row(a0),
      wcomb.astype(BF16), row(k_k), row(k_a), row(r_k), row(gn_w), row(gn_b), w_out)


def kernel(x, ffn1_norm, ffn1_w_gate, ffn1_w_up, ffn1_w_down, mix_norm, w_in, hgrn_lb_logits, hgrn_out_norm, rwkv_shift_mu, rwkv_w0, rwkv_w2, rwkv_a0, rwkv_a2, rwkv_g2, rwkv_k_k, rwkv_k_a, rwkv_r_k, rwkv_gn_w, rwkv_gn_b, w_out, ffn2_norm, ffn2_w_gate, ffn2_w_up, ffn2_w_down, final_norm):
    bsz, t, d = x.shape
    depth = ffn1_norm.shape[0]
    assert depth == 1 and hgrn_lb_logits.shape[0] == depth + 1
    l = 0
    row = lambda vec: vec.reshape(1, -1).astype(F32)
    fg = row(final_norm)
    h = _ffn(x.reshape(bsz * t, d), row(ffn1_norm[l]), ffn1_w_gate[l], ffn1_w_up[l], ffn1_w_down[l],
             fg, final_norm=False)
    h = _mixer(h.reshape(bsz, t, d), mix_norm[l], w_in[l], hgrn_lb_logits, hgrn_out_norm[l],
               rwkv_shift_mu[l], rwkv_w0[l], rwkv_w2[l], rwkv_a0[l], rwkv_a2[l], rwkv_g2[l],
               rwkv_k_k[l], rwkv_k_a[l], rwkv_r_k[l], rwkv_gn_w[l], rwkv_gn_b[l], w_out[l])
    h = _ffn(h.reshape(bsz * t, d), row(ffn2_norm[l]), ffn2_w_gate[l], ffn2_w_up[l], ffn2_w_down[l],
             fg, final_norm=True)
    return h.reshape(bsz, t, d)
```

```python
import functools
import math

import jax
import jax.numpy as jnp
from jax import lax
from jax.experimental import pallas as pl
from jax.experimental.pallas import tpu as pltpu

F32 = jnp.float32
BF16 = jnp.bfloat16

NORM_EPS = 1e-6
RWKV_GN_EPS = 64e-5
L2_EPS = 1e-12
DECAY_SCALE = math.exp(-0.5)

LANES = 128
MXU_COLS = 256
HA_HEAD_DIM = 128
HB_HEAD_DIM = 64
CHUNK = 64
DECAY_LORA, AAA_LORA, GATE_LORA = 32, 32, 96
LORA_PAD = 256
FFN_ROW_TILE = 1024
FFN_COL_TILE = 768
FFN_WEIGHT_SLABS = 8
MIXER_WEIGHT_SLABS = 8
MIXER_GROUP = 4
MIXER_GROUPS_PER_STEP = 2
IN_PROJ_COL_TILE = 512
MIB = 1024 * 1024


def _dot(a, b):
    return jnp.dot(a, b, preferred_element_type=F32)


def _dot_nt(a, b):
    return lax.dot_general(a, b, (((1,), (1,)), ((), ())), preferred_element_type=F32)


def _dot_tn(a, b):
    return lax.dot_general(a, b, (((0,), (0,)), ((), ())), preferred_element_type=F32)


def _split_bf16(x):
    hi = x.astype(BF16)
    lo = (x - hi.astype(F32)).astype(BF16)
    return hi, lo


def _rms_norm(x, g):
    return x * lax.rsqrt(jnp.mean(x * x, axis=-1, keepdims=True) + NORM_EPS) * g


def _sigmoid(x):
    return 0.5 * jnp.tanh(0.5 * x) + 0.5


def _ffn_kernel(x_ref, g_ref, wg_ref, wu_ref, wd_ref, fn_ref, o_ref, wg_s, wu_s, wd_s, *,
                col_tiles, final_norm, n_load):
    i = pl.program_id(0)

    @pl.when(i < n_load)
    def _():
        rg, rd = wg_ref.shape[0], wd_ref.shape[0]
        wg_s[pl.ds(pl.multiple_of(i * rg, rg), rg), :] = wg_ref[...].astype(BF16)
        wu_s[pl.ds(pl.multiple_of(i * rg, rg), rg), :] = wu_ref[...].astype(BF16)
        wd_s[pl.ds(pl.multiple_of(i * rd, rd), rd), :] = wd_ref[...].astype(BF16)

    @pl.when(i >= n_load)
    def _():
        x = x_ref[...]
        h = _rms_norm(x, g_ref[...]).astype(BF16)
        acc = jnp.zeros(x.shape, F32)
        for lo, hi in col_tiles:
            gate = _dot(h, wg_s[:, lo:hi])
            up = _dot(h, wu_s[:, lo:hi])
            act = (gate * _sigmoid(gate) * up).astype(BF16)
            acc = acc + _dot(act, wd_s[lo:hi, :])
        y = x + 0.5 * acc
        if final_norm:
            y = _rms_norm(y, fn_ref[...])
        o_ref[...] = y


def _resident(shape):
    return pl.BlockSpec(shape, lambda *_: (0,) * len(shape), pipeline_mode=pl.Buffered(1))


def _ffn(x2d, norm_g, w_gate, w_up, w_down, final_g, *, final_norm):
    n, d = x2d.shape
    d_ff = w_gate.shape[1]
    tm, n_load = FFN_ROW_TILE, FFN_WEIGHT_SLABS
    assert n % tm == 0 and d_ff % MXU_COLS == 0
    assert d % (16 * n_load) == 0 and d_ff % (16 * n_load) == 0
    col_tiles = tuple((lo, min(lo + FFN_COL_TILE, d_ff)) for lo in range(0, d_ff, FFN_COL_TILE))
    weight_bytes = 3 * d * d_ff * 2
    slab_bytes = 3 * d * d_ff * 4 // n_load
    tile_bytes = tm * d * 4
    vmem = weight_bytes + 2 * slab_bytes + 8 * tile_bytes + 3 * tm * FFN_COL_TILE * 4 + 8 * MIB
    slab = lambda i: (jnp.minimum(i, n_load - 1), 0)
    tile = lambda i: (jnp.maximum(i - n_load, 0), 0)
    return pl.pallas_call(
        functools.partial(_ffn_kernel, col_tiles=col_tiles, final_norm=final_norm, n_load=n_load),
        grid=(n_load + n // tm,),
        in_specs=[
            pl.BlockSpec((tm, d), tile),
            _resident((1, d)),
            pl.BlockSpec((d // n_load, d_ff), slab),
            pl.BlockSpec((d // n_load, d_ff), slab),
            pl.BlockSpec((d_ff // n_load, d), slab),
            _resident((1, d)),
        ],
        out_specs=pl.BlockSpec((tm, d), tile),
        out_shape=jax.ShapeDtypeStruct((n, d), F32),
        scratch_shapes=[
            pltpu.VMEM((d, d_ff), BF16),
            pltpu.VMEM((d, d_ff), BF16),
            pltpu.VMEM((d_ff, d), BF16),
        ],
        compiler_params=pltpu.CompilerParams(
            dimension_semantics=("arbitrary",), vmem_limit_bytes=vmem),
        name="ffn_final" if final_norm else "ffn",
    )(x2d, norm_g, w_gate, w_up, w_down, final_g)


def _iota2(shape, axis):
    return lax.broadcasted_iota(jnp.int32, shape, axis)


def _segsum(x, lane_lo):
    outs = []
    for j in range(x.shape[1] // LANES):
        t = x[:, j * LANES:(j + 1) * LANES]
        lo = jnp.sum(jnp.where(lane_lo, t, 0.0), axis=-1, keepdims=True)
        hi = jnp.sum(jnp.where(lane_lo, 0.0, t), axis=-1, keepdims=True)
        outs.append(jnp.where(lane_lo, lo, hi))
    return jnp.concatenate(outs, axis=1)


def _block_rows(x, idx, fn=None):
    c = CHUNK
    outs = []
    for r in range(0, x.shape[0], c):
        row = x[r + idx:r + idx + 1]
        if fn is not None:
            row = fn(row, x[r + c // 2:r + c // 2 + 1])
        outs.append(jnp.broadcast_to(row, (c, x.shape[1])))
    return jnp.concatenate(outs, axis=0)


def _tile(x, b, j, width):
    return x[b * CHUNK:(b + 1) * CHUNK, j * width:(j + 1) * width]


def _assemble(tiles, nb, nj):
    return jnp.concatenate(
        [jnp.concatenate([tiles[(b, j)] for j in range(nj)], axis=1) for b in range(nb)], axis=0)


def _merge_stages(*stage_lists):
    tagged = []
    for stages in stage_lists:
        tagged += [((i + 0.5) / len(stages), i, fn) for i, fn in enumerate(stages)]
    for _, _, fn in sorted(tagged, key=lambda t: t[:2]):
        fn()


class _Masks:
    def __init__(self, rows):
        c = CHUNK
        ri, ci = _iota2((c, c), 0), _iota2((c, c), 1)
        self.causal = ci <= ri
        bi, bj = _iota2((rows, rows), 0), _iota2((rows, rows), 1)
        same_block = lax.shift_right_logical(bi, 6) == lax.shift_right_logical(bj, 6)
        tri = ((bj <= bi) & same_block).astype(BF16)
        self.tri2 = jnp.concatenate([tri, tri], axis=1)
        gi, gj = _iota2((2 * c, 4 * c), 0), _iota2((2 * c, 4 * c), 1)
        ti, sj = gi & (c - 1), gj & (c - 1)
        self.gmask = (sj < ti) | ((gi >= c) & (sj == ti))
        li, lj = _iota2((LANES, LANES), 0), _iota2((LANES, LANES), 1)
        self.bdmask = lax.shift_right_logical(li, 6) == lax.shift_right_logical(lj, 6)
        self.lane_lo = _iota2((1, LANES), 1) < HB_HEAD_DIM
        self.row0 = (_iota2((rows, 1), 0) & (c - 1)) == 0
        self.lora_lane = _iota2((1, LORA_PAD), 1)


def _hgrn_stages(ctx, hs_ref, b0, nb, w_a, mk):
    hd = HA_HEAD_DIM
    chains = [(b, h) for b in range(nb) for h in range(w_a // hd)]
    st = {}

    def scores():
        qt, kt = ctx['hg_qt'], ctx['hg_kt']
        st['s_old'] = {ch: hs_ref[b0 + ch[0], ch[1]] for ch in chains}
        st['sc'] = {ch: jnp.where(mk.causal, _dot_nt(_tile(qt, *ch, hd), _tile(kt, *ch, hd)), 0.0).astype(BF16)
                    for ch in chains}

    def outputs():
        v, q0, kl, d = ctx['hg_v'], ctx['hg_q0'], ctx['hg_kl'], ctx['hg_d']
        s_old = st['s_old']
        o = {ch: _dot(st['sc'][ch], _tile(v, *ch, hd)) + _dot_nt(_tile(q0, *ch, hd), s_old[ch].astype(BF16))
             for ch in chains}
        for ch in chains:
            hs_ref[b0 + ch[0], ch[1]] = (_tile(d, *ch, hd)[0:1] * s_old[ch]
                                         + _dot_tn(_tile(v, *ch, hd), _tile(kl, *ch, hd)))
        st['o'] = o

    def normalise():
        o = {ch: t * lax.rsqrt(jnp.mean(t * t, axis=-1, keepdims=True) + NORM_EPS) for ch, t in st['o'].items()}
        ctx['o_a'] = _assemble(o, nb, w_a // hd)

    return [scores, outputs, normalise]


def _rwkv_stages(ctx, rs_ref, b0, nb, w_b, mk):
    c = CHUNK
    n_pairs = w_b // LANES
    lane_lo = mk.lane_lo
    pairs = [(b, pr) for b in range(nb) for pr in range(n_pairs)]
    tl = lambda t, pair: _tile(t, pair[0], pair[1], LANES)
    zero = jnp.zeros((c, LANES), BF16)
    lo = lambda t: jnp.where(lane_lo, t, zero)
    hi = lambda t: jnp.where(lane_lo, zero, t)
    cat0 = lambda *ts: jnp.concatenate(ts, axis=0)
    st = {}

    def state_read():
        at, rt, bt, kt, a0, r0, bl, kl, v = ctx['rw']
        st['s_old'] = {pp: rs_ref[b0 + pp[0], pp[1]] for pp in pairs}
        st['z'] = {pp: _dot_nt(cat0(tl(a0, pp), tl(r0, pp)), st['s_old'][pp].astype(BF16)) for pp in pairs}

    def gram():
        at, rt, bt, kt, a0, r0, bl, kl, v = ctx['rw']
        g = {}
        for pp in pairs:
            b_, k_ = tl(bt, pp), tl(kt, pp)
            g[pp] = jnp.where(
                mk.gmask, _dot_nt(cat0(tl(at, pp), tl(rt, pp)), cat0(lo(b_), hi(b_), lo(k_), hi(k_))), 0.0)
        st['g'] = g

    def rhs0():
        v = ctx['rw'][8]
        x, pw = {}, {}
        for pp in pairs:
            pw[pp] = st['g'][pp][:c, :LANES].astype(BF16)
            lak = st['g'][pp][:c, LANES:].astype(BF16)
            x[pp] = st['z'][pp][:c] + _dot(lak, cat0(lo(tl(v, pp)), hi(tl(v, pp))))
        st['x'], st['pw'] = x, pw

    def solve_step():
        x, pw, res = st['x'], st['pw'], {}
        for pp in pairs:
            xb = x[pp].astype(BF16)
            rhs = jnp.concatenate([cat0(lo(xb), hi(xb)), cat0(lo(pw[pp]), hi(pw[pp]))], axis=1)
            res[pp] = _dot(pw[pp], rhs)
        st['x'] = {pp: x[pp] + res[pp][:, :LANES] for pp in pairs}
        st['pw'] = {pp: res[pp][:, LANES:].astype(BF16) for pp in pairs}

    def solve_last():
        x, pw = st['x'], st['pw']
        for pp in pairs:
            xb = x[pp].astype(BF16)
            x[pp] = x[pp] + _dot(pw[pp], cat0(lo(xb), hi(xb)))

    def outputs():
        at, rt, bt, kt, a0, r0, bl, kl, v = ctx['rw']
        dl = ctx['rw_dl']
        y = {}
        for pp in pairs:
            ub, vv = st['x'][pp].astype(BF16), tl(v, pp)
            y[pp] = st['z'][pp][c:] + _dot(st['g'][pp][c:].astype(BF16), cat0(lo(ub), hi(ub), lo(vv), hi(vv)))
            upd = _dot_tn(cat0(ub, vv), cat0(tl(bl, pp), tl(kl, pp)))
            rs_ref[b0 + pp[0], pp[1]] = tl(dl, pp)[0:1] * st['s_old'][pp] + jnp.where(mk.bdmask, upd, 0.0)
        ctx['y'] = _assemble(y, nb, n_pairs)

    n_steps = 0
    n = 1
    while 2 * n < c:
        n_steps += 1
        n *= 2
    return [state_read, gram, rhs0] + [solve_step] * n_steps + [solve_last, outputs]


def _mixer_kernel(x_ref, xn_ref, mn_ref, win_f32, lbl_ref, hon_ref, mu_ref, w0_ref, a0_ref, wcomb_ref,
                  kk_ref, ka_ref, rk_ref, gnw_ref, gnb_ref, wout_f32, o_ref,
                  p_ref, hs_ref, rs_ref, carry_ref, win_s, wout_s, *, n_load, **dims):
    j = pl.program_id(1)

    @pl.when(j < n_load)
    def _():
        ri, ro = win_f32.shape[0], wout_f32.shape[0]
        n_in = win_f32.shape[1]
        rows_in = pl.ds(pl.multiple_of(j * ri, ri), ri)
        win_s[rows_in, 0:n_in] = win_f32[...].astype(BF16)
        win_s[rows_in, n_in:] = jnp.zeros((ri, win_s.shape[1] - n_in), BF16)
        wout_s[pl.ds(pl.multiple_of(j * ro, ro), ro), :] = wout_f32[...].astype(BF16)

    @pl.when(j >= n_load)
    def _():
        _mixer_step(x_ref, xn_ref, mn_ref, win_s, lbl_ref, hon_ref, mu_ref, w0_ref, a0_ref, wcomb_ref,
                    kk_ref, ka_ref, rk_ref, gnw_ref, gnb_ref, wout_s, o_ref,
                    p_ref, hs_ref, rs_ref, carry_ref, first_step=n_load, **dims)


def _mixer_step(x_ref, xn_ref, mn_ref, win_ref, lbl_ref, hon_ref, mu_ref, w0_ref, a0_ref, wcomb_ref,
                kk_ref, ka_ref, rk_ref, gnw_ref, gnb_ref, wout_ref, o_ref,
                p_ref, hs_ref, rs_ref, carry_ref, *, first_step, nb, n_groups, w_a, w_b):
    c = CHUNK
    rows = nb * c
    d_model = x_ref.shape[-1]
    n_rkv = 3 * w_b + LORA_PAD

    def in_proj_stages(grp, load_x):
        prow = slice(grp * rows, (grp + 1) * rows)
        hn = {}

        def norm():
            x = load_x().reshape(rows, d_model)
            hn['h'] = _rms_norm(x, mn_ref[...]).astype(BF16)

        def cols(lo, hi):
            def project():
                p_ref[prow, lo:hi] = _dot(hn['h'], win_ref[:, lo:hi])
            return project

        n_cols = 4 * w_a + n_rkv
        return [norm] + [cols(lo, min(lo + IN_PROJ_COL_TILE, n_cols))
                         for lo in range(0, n_cols, IN_PROJ_COL_TILE)]

    group_x = lambda grp: (lambda: x_ref[grp * nb:(grp + 1) * nb])

    @pl.when(pl.program_id(1) == first_step)
    def _():
        hs_ref[...] = jnp.zeros_like(hs_ref)
        rs_ref[...] = jnp.zeros_like(rs_ref)
        carry_ref[...] = jnp.zeros_like(carry_ref)
        _merge_stages(in_proj_stages(0, group_x(0)))

    mk = _Masks(c)
    logits = lbl_ref[...]
    e = jnp.exp(logits - jnp.max(logits, axis=0, keepdims=True))
    lb = e[0:1] / jnp.sum(e, axis=0, keepdims=True)

    def prep_stages(grp, ctx):
        bctxs = [dict() for _ in range(nb)]
        per_batch = [prep_batch_stages(grp * nb + b, bctxs[b]) for b in range(nb)]

        def collect():
            rows_of = lambda name: jnp.concatenate([bc_[name] for bc_ in bctxs], axis=0)
            for name in ('hg_qt', 'hg_kt', 'hg_kl', 'hg_q0', 'hg_d', 'hg_v', 'hg_gate', 'bonus', 'gate', 'rw_dl'):
                ctx[name] = rows_of(name)
            ctx['rw'] = tuple(jnp.concatenate([bc_['rw'][i] for bc_ in bctxs], axis=0) for i in range(9))

        return [fn for stage in zip(*per_batch) for fn in stage] + [collect]

    def prep_batch_stages(bidx, ctx):
        prow = slice(bidx * c, (bidx + 1) * c)

        def hgrn_gates():
            q_a = p_ref[prow, 0:w_a]
            f_a = p_ref[prow, w_a:2 * w_a]
            forget = lb + (1.0 - lb) * _sigmoid(f_a)
            ctx['hg_logf'] = _split_bf16(jnp.log(forget))
            ctx['hg_kh'] = 1.0 - forget
            ctx['hg_q'] = q_a * _sigmoid(q_a)
            g_a = p_ref[prow, 3 * w_a:4 * w_a]
            ctx['hg_gate'] = hon_ref[...] * (g_a * _sigmoid(g_a))

        def hgrn_decays():
            hi, lo = ctx['hg_logf']
            bc = _dot(mk.tri2, jnp.concatenate([hi, lo], axis=0))
            q, kh = ctx['hg_q'], ctx['hg_kh']
            b_ref = _block_rows(bc, c // 2)
            qt = q * jnp.exp(bc - b_ref)
            kt = kh * jnp.exp(b_ref - bc)
            ctx['hg_qt'] = qt.astype(BF16)
            ctx['hg_kt'] = kt.astype(BF16)
            ctx['hg_kl'] = (kt * _block_rows(bc, c - 1, lambda last, ref: jnp.exp(last - ref))).astype(BF16)
            ctx['hg_q0'] = (qt * _block_rows(bc, c - 1, lambda last, ref: jnp.exp(ref))).astype(BF16)
            ctx['hg_d'] = _block_rows(bc, c - 1, lambda last, ref: jnp.exp(last))
            ctx['hg_v'] = p_ref[prow, 2 * w_a:3 * w_a].astype(BF16)

        def rwkv_shift():
            pb = p_ref[prow, 4 * w_a:4 * w_a + n_rkv]
            prev = jnp.broadcast_to(carry_ref[bidx:bidx + 1, :], (c, n_rkv))
            shifted = jnp.where(mk.row0, prev, pltpu.roll(pb, 1, 0))
            carry_ref[bidx:bidx + 1, :] = pb[c - 1:c, :]
            pb = pb + mu_ref[...] * (shifted - pb)
            ctx['r'] = pb[:, 0:w_b]
            ctx['k'] = pb[:, w_b:2 * w_b]
            ctx['v'] = pb[:, 2 * w_b:3 * w_b]
            low = pb[:, 3 * w_b:]
            ll = mk.lora_lane
            act = jnp.where(ll < DECAY_LORA, jnp.tanh(low),
                            jnp.where(ll < DECAY_LORA + AAA_LORA, low,
                                      jnp.where(ll < DECAY_LORA + AAA_LORA + GATE_LORA, _sigmoid(low), 0.0)))
            ctx['act'] = act.astype(BF16)

        def rwkv_lora():
            r, k = ctx['r'], ctx['k']
            ld = _dot(ctx['act'], wcomb_ref[...])
            lw = -DECAY_SCALE * _sigmoid(w0_ref[...] + ld[:, 0:w_b])
            a = _sigmoid(a0_ref[...] + ld[:, w_b:2 * w_b])
            ctx['gate'] = ld[:, 2 * w_b:3 * w_b]
            kk = k * kk_ref[...]
            k = k * (1.0 + (a - 1.0) * ka_ref[...])
            ctx['k'], ctx['a'], ctx['kk'], ctx['lw'] = k, a, kk, lw
            ctx['lw_split'] = _split_bf16(lw)
            ctx['kk_sq'] = kk * kk
            ctx['rk'] = r * k * rk_ref[...]

        def rwkv_sums():
            hi, lo = ctx['lw_split']
            ctx['gc'] = _dot(mk.tri2, jnp.concatenate([hi, lo], axis=0))
            kk = ctx['kk'] * lax.rsqrt(jnp.maximum(_segsum(ctx['kk_sq'], mk.lane_lo), L2_EPS * L2_EPS))
            ctx['bonus'] = _segsum(ctx['rk'], mk.lane_lo) * ctx['v']
            ctx['av'], ctx['bv'] = -kk, kk * ctx['a']

        def rwkv_decays():
            r, k, v, av, bv, lw, gc = (ctx[n] for n in ('r', 'k', 'v', 'av', 'bv', 'lw', 'gc'))
            gref = _block_rows(gc, c // 2)
            e_up = jnp.exp(gc - gref)
            e_dn = jnp.exp(gref - gc)
            at, rt, bt, kt = av * jnp.exp(gc - lw - gref), r * e_up, bv * e_dn, k * e_dn
            s_ref = _block_rows(gc, c - 1, lambda last, ref: jnp.exp(ref))
            s_last = _block_rows(gc, c - 1, lambda last, ref: jnp.exp(last - ref))
            full = (at, rt, bt, kt, at * s_ref, rt * s_ref, bt * s_last, kt * s_last, v)
            ctx['rw'] = tuple(t.astype(BF16) for t in full)
            ctx['rw_dl'] = _block_rows(gc, c - 1, lambda last, ref: jnp.exp(last))

        return [hgrn_gates, hgrn_decays, rwkv_shift, rwkv_lora, rwkv_sums, rwkv_decays]

    def chain_stages(grp, ctx):
        b0 = grp * nb
        return _hgrn_stages(ctx, hs_ref, b0, nb, w_a, mk) + _rwkv_stages(ctx, rs_ref, b0, nb, w_b, mk)

    def post_stages(grp, ctx):
        b0 = grp * nb
        inv_n = 1.0 / HB_HEAD_DIM
        parts = [dict() for _ in range(nb)]

        def batch_stages(b, st):
            brow = slice(b * c, (b + 1) * c)

            def gate_a():
                st['o_a'] = (ctx['o_a'][brow] * ctx['hg_gate'][brow]).astype(BF16)
                st['y'] = ctx['y'][brow]

            def gn_mean():
                yc = st['y'] - _segsum(st['y'], mk.lane_lo) * inv_n
                st['yc'], st['yc_sq'] = yc, yc * yc

            def gn_var():
                var = _segsum(st['yc_sq'], mk.lane_lo) * inv_n
                yn = st['yc'] * lax.rsqrt(var + RWKV_GN_EPS) * gnw_ref[...] + gnb_ref[...]
                o_b = ((yn + ctx['bonus'][brow]) * ctx['gate'][brow]).astype(BF16)
                st['o'] = jnp.concatenate([st['o_a'], o_b], axis=1)

            return [gate_a, gn_mean, gn_var]

        def out_proj():
            o = jnp.concatenate([st['o'] for st in parts], axis=0)
            x = x_ref[b0:b0 + nb].reshape(rows, d_model)
            o_ref[b0:b0 + nb] = (x + _dot(o, wout_ref[...])).reshape(nb, c, d_model)

        per_batch = [batch_stages(b, parts[b]) for b in range(nb)]
        return [fn for stage in zip(*per_batch) for fn in stage] + [out_proj]

    ctxs = [dict() for _ in range(n_groups)]
    lists = [prep_stages(0, ctxs[0])]
    if n_groups > 1:
        lists.append(in_proj_stages(1, group_x(1)))
    _merge_stages(*lists)
    next_proj = in_proj_stages(0, lambda: xn_ref[...])
    n_early = 1 + (len(next_proj) - 1) // 2
    for grp in range(n_groups):
        lists = [chain_stages(grp, ctxs[grp])]
        if grp + 1 < n_groups:
            lists.append(prep_stages(grp + 1, ctxs[grp + 1]))
        if grp + 2 < n_groups:
            lists.append(in_proj_stages(grp + 2, group_x(grp + 2)))
        if grp >= 1:
            lists.append(post_stages(grp - 1, ctxs[grp - 1]))
        if grp == 0:
            lists.append(next_proj[:n_early])
        _merge_stages(*lists)
    _merge_stages(post_stages(n_groups - 1, ctxs[n_groups - 1]), next_proj[n_early:])


def _mixer(x, mix_norm, w_in, lb_logits, hgrn_out_norm, mu, w0, w2, a0, a2, g2, k_k, k_a, r_k,
           gn_w, gn_b, w_out, *, layer):
    bsz, t, d = x.shape
    w_a = hgrn_out_norm.shape[-1]
    w_b = w0.shape[-1]
    nb, n_groups, c = MIXER_GROUP, MIXER_GROUPS_PER_STEP, CHUNK
    nbs = nb * n_groups
    n_lora = DECAY_LORA + AAA_LORA + GATE_LORA
    assert bsz % nbs == 0 and t % c == 0
    assert w_a % HA_HEAD_DIM == 0 and w_b % LANES == 0
    assert w_in.shape[-1] == 4 * w_a + 3 * w_b + n_lora
    pad = LORA_PAD - n_lora
    n_cols = 4 * w_a + 3 * w_b + LORA_PAD
    n_rkv = 3 * w_b + LORA_PAD
    mu_p = jnp.pad(mu.reshape(1, -1), ((0, 0), (0, pad)))
    wcomb = jnp.zeros((LORA_PAD, 3 * w_b), F32)
    wcomb = wcomb.at[0:DECAY_LORA, 0:w_b].set(w2)
    wcomb = wcomb.at[DECAY_LORA:DECAY_LORA + AAA_LORA, w_b:2 * w_b].set(a2)
    wcomb = wcomb.at[DECAY_LORA + AAA_LORA:n_lora, 2 * w_b:3 * w_b].set(g2)
    row = lambda vec: vec.reshape(1, -1).astype(F32)
    n_pairs = w_b // LANES
    n_heads_a = w_a // HA_HEAD_DIM
    state_bytes = nbs * (n_heads_a * HA_HEAD_DIM * HA_HEAD_DIM + n_pairs * LANES * LANES) * 4
    weight_bytes = (d * n_cols + LORA_PAD * 3 * w_b + (w_a + w_b) * d) * 2
    n_load, nt = MIXER_WEIGHT_SLABS, t // c
    assert bsz == nbs and d % (16 * n_load) == 0 and (w_a + w_b) % (16 * n_load) == 0
    slab_bytes = (d * w_in.shape[-1] + (w_a + w_b) * d) * 4 // n_load
    vmem = (weight_bytes + 2 * slab_bytes + state_bytes + 3 * nbs * c * n_cols * 4 + 4 * nbs * c * d * 4
            + 12 * MIB)
    slab = lambda i, j: (jnp.minimum(j, n_load - 1), 0)
    chunk = lambda i, j: (i, jnp.maximum(j - n_load, 0), 0)
    return pl.pallas_call(
        functools.partial(_mixer_kernel, n_load=n_load, nb=nb, n_groups=n_groups, w_a=w_a, w_b=w_b),
        grid=(1, n_load + nt),
        in_specs=[
            pl.BlockSpec((nbs, c, d), chunk),
            pl.BlockSpec((nb, c, d), lambda i, j: (0, jnp.clip(j - n_load + 1, 0, nt - 1), 0)),
            _resident((1, d)),
            pl.BlockSpec((None, d // n_load, w_in.shape[-1]),
                         lambda i, j: (layer, jnp.minimum(j, n_load - 1), 0)),
            _resident(lb_logits.shape),
            _resident((1, w_a)),
            _resident((1, n_rkv)),
            _resident((1, w_b)),
            _resident((1, w_b)),
            _resident((LORA_PAD, 3 * w_b)),
            _resident((1, w_b)),
            _resident((1, w_b)),
            _resident((1, w_b)),
            _resident((1, w_b)),
            _resident((1, w_b)),
            pl.BlockSpec(((w_a + w_b) // n_load, d), slab),
        ],
        out_specs=pl.BlockSpec((nbs, c, d), chunk),
        out_shape=jax.ShapeDtypeStruct((bsz, t, d), F32),
        scratch_shapes=[
            pltpu.VMEM((nbs * c, n_cols), F32),
            pltpu.VMEM((nbs, n_heads_a, HA_HEAD_DIM, HA_HEAD_DIM), F32),
            pltpu.VMEM((nbs, n_pairs, LANES, LANES), F32),
            pltpu.VMEM((nbs, n_rkv), F32),
            pltpu.VMEM((d, n_cols), BF16),
            pltpu.VMEM((w_a + w_b, d), BF16),
        ],
        compiler_params=pltpu.CompilerParams(
            dimension_semantics=("arbitrary", "arbitrary"), vmem_limit_bytes=vmem),
        name="mixer",
    )(x, x, row(mix_norm), w_in, lb_logits.astype(F32), row(hgrn_out_norm), mu_p, row(w0), row(a0),
      wcomb.astype(BF16), row(k_k), row(k_a), row(r_k), row(gn_w), row(gn_b), w_out)


def kernel(x, ffn1_norm, ffn1_w_gate, ffn1_w_up, ffn1_w_down, mix_norm, w_in, hgrn_lb_logits, hgrn_out_norm, rwkv_shift_mu, rwkv_w0, rwkv_w2, rwkv_a0, rwkv_a2, rwkv_g2, rwkv_k_k, rwkv_k_a, rwkv_r_k, rwkv_gn_w, rwkv_gn_b, w_out, ffn2_norm, ffn2_w_gate, ffn2_w_up, ffn2_w_down, final_norm):
    bsz, t, d = x.shape
    depth = ffn1_norm.shape[0]
    assert depth == 1 and hgrn_lb_logits.shape[0] == depth + 1
    l = 0
    row = lambda vec: vec.reshape(1, -1).astype(F32)
    fg = row(final_norm)
    h = _ffn(x.reshape(bsz * t, d), row(ffn1_norm[l]), ffn1_w_gate[l], ffn1_w_up[l], ffn1_w_down[l],
             fg, final_norm=False)
    h = _mixer(h.reshape(bsz, t, d), mix_norm[l], w_in, hgrn_lb_logits, hgrn_out_norm[l],
               rwkv_shift_mu[l], rwkv_w0[l], rwkv_w2[l], rwkv_a0[l], rwkv_a2[l], rwkv_g2[l],
               rwkv_k_k[l], rwkv_k_a[l], rwkv_r_k[l], rwkv_gn_w[l], rwkv_gn_b[l], w_out[l], layer=l)
    h = _ffn(h.reshape(bsz * t, d), row(ffn2_norm[l]), ffn2_w_gate[l], ffn2_w_up[l], ffn2_w_down[l],
             fg, final_norm=True)
    return h.reshape(bsz, t, d)
```

```python
import functools
import math

import jax
import jax.numpy as jnp
from jax import lax
from jax.experimental import pallas as pl
from jax.experimental.pallas import tpu as pltpu

F32 = jnp.float32
BF16 = jnp.bfloat16

NORM_EPS = 1e-6
RWKV_GN_EPS = 64e-5
L2_EPS = 1e-12
DECAY_SCALE = math.exp(-0.5)

LANES = 128
MXU_COLS = 256
HA_HEAD_DIM = 128
HB_HEAD_DIM = 64
CHUNK = 64
DECAY_LORA, AAA_LORA, GATE_LORA = 32, 32, 96
LORA_PAD = 256
FFN_ROW_TILE = 1024
FFN_COL_TILE = 768
FFN_WEIGHT_SLABS = 8
MIXER_WEIGHT_SLABS = 8
MIXER_GROUP = 4
MIXER_GROUPS_PER_STEP = 2
IN_PROJ_COL_TILE = 512
MIB = 1024 * 1024


def _dot(a, b):
    return jnp.dot(a, b, preferred_element_type=F32)


def _dot_nt(a, b):
    return lax.dot_general(a, b, (((1,), (1,)), ((), ())), preferred_element_type=F32)


def _dot_tn(a, b):
    return lax.dot_general(a, b, (((0,), (0,)), ((), ())), preferred_element_type=F32)


def _split_bf16(x):
    hi = x.astype(BF16)
    lo = (x - hi.astype(F32)).astype(BF16)
    return hi, lo


def _rms_norm(x, g):
    return x * lax.rsqrt(jnp.mean(x * x, axis=-1, keepdims=True) + NORM_EPS) * g


def _sigmoid(x):
    return 0.5 * jnp.tanh(0.5 * x) + 0.5


def _ffn_kernel(x_ref, g_ref, wg_ref, wu_ref, wd_ref, fn_ref, o_ref, wg_s, wu_s, wd_s, *,
                col_tiles, final_norm, n_load):
    i = pl.program_id(0)

    @pl.when(i < n_load)
    def _():
        rg, rd = wg_ref.shape[0], wd_ref.shape[0]
        wg_s[pl.ds(pl.multiple_of(i * rg, rg), rg), :] = wg_ref[...].astype(BF16)
        wu_s[pl.ds(pl.multiple_of(i * rg, rg), rg), :] = wu_ref[...].astype(BF16)
        wd_s[pl.ds(pl.multiple_of(i * rd, rd), rd), :] = wd_ref[...].astype(BF16)

    @pl.when(i >= n_load)
    def _():
        x = x_ref[...]
        h = _rms_norm(x, g_ref[...]).astype(BF16)
        acc = jnp.zeros(x.shape, F32)
        for lo, hi in col_tiles:
            gate = _dot(h, wg_s[:, lo:hi])
            up = _dot(h, wu_s[:, lo:hi])
            act = (gate * _sigmoid(gate) * up).astype(BF16)
            acc = acc + _dot(act, wd_s[lo:hi, :])
        y = x + 0.5 * acc
        if final_norm:
            y = _rms_norm(y, fn_ref[...])
        o_ref[...] = y


def _resident(shape):
    return pl.BlockSpec(shape, lambda *_: (0,) * len(shape), pipeline_mode=pl.Buffered(1))


def _ffn(x2d, norm_g, w_gate, w_up, w_down, final_g, *, final_norm):
    n, d = x2d.shape
    d_ff = w_gate.shape[1]
    tm, n_load = FFN_ROW_TILE, FFN_WEIGHT_SLABS
    assert n % tm == 0 and d_ff % MXU_COLS == 0
    assert d % (16 * n_load) == 0 and d_ff % (16 * n_load) == 0
    col_tiles = tuple((lo, min(lo + FFN_COL_TILE, d_ff)) for lo in range(0, d_ff, FFN_COL_TILE))
    weight_bytes = 3 * d * d_ff * 2
    slab_bytes = 3 * d * d_ff * 4 // n_load
    tile_bytes = tm * d * 4
    vmem = weight_bytes + 2 * slab_bytes + 8 * tile_bytes + 3 * tm * FFN_COL_TILE * 4 + 8 * MIB
    slab = lambda i: (jnp.minimum(i, n_load - 1), 0)
    tile = lambda i: (jnp.maximum(i - n_load, 0), 0)
    return pl.pallas_call(
        functools.partial(_ffn_kernel, col_tiles=col_tiles, final_norm=final_norm, n_load=n_load),
        grid=(n_load + n // tm,),
        in_specs=[
            pl.BlockSpec((tm, d), tile),
            _resident((1, d)),
            pl.BlockSpec((d // n_load, d_ff), slab),
            pl.BlockSpec((d // n_load, d_ff), slab),
            pl.BlockSpec((d_ff // n_load, d), slab),
            _resident((1, d)),
        ],
        out_specs=pl.BlockSpec((tm, d), tile),
        out_shape=jax.ShapeDtypeStruct((n, d), F32),
        scratch_shapes=[
            pltpu.VMEM((d, d_ff), BF16),
            pltpu.VMEM((d, d_ff), BF16),
            pltpu.VMEM((d_ff, d), BF16),
        ],
        compiler_params=pltpu.CompilerParams(
            dimension_semantics=("arbitrary",), vmem_limit_bytes=vmem),
        name="ffn_final" if final_norm else "ffn",
    )(x2d, norm_g, w_gate, w_up, w_down, final_g)


def _iota2(shape, axis):
    return lax.broadcasted_iota(jnp.int32, shape, axis)


def _segsum(x, lane_lo):
    outs = []
    for j in range(x.shape[1] // LANES):
        t = x[:, j * LANES:(j + 1) * LANES]
        lo = jnp.sum(jnp.where(lane_lo, t, 0.0), axis=-1, keepdims=True)
        hi = jnp.sum(jnp.where(lane_lo, 0.0, t), axis=-1, keepdims=True)
        outs.append(jnp.where(lane_lo, lo, hi))
    return jnp.concatenate(outs, axis=1)


def _block_rows(x, idx, fn=None):
    c = CHUNK
    outs = []
    for r in range(0, x.shape[0], c):
        row = x[r + idx:r + idx + 1]
        if fn is not None:
            row = fn(row, x[r + c // 2:r + c // 2 + 1])
        outs.append(jnp.broadcast_to(row, (c, x.shape[1])))
    return jnp.concatenate(outs, axis=0)


def _tile(x, b, j, width):
    return x[b * CHUNK:(b + 1) * CHUNK, j * width:(j + 1) * width]


def _assemble(tiles, nb, nj):
    return jnp.concatenate(
        [jnp.concatenate([tiles[(b, j)] for j in range(nj)], axis=1) for b in range(nb)], axis=0)


def _merge_stages(*stage_lists):
    tagged = []
    for stages in stage_lists:
        tagged += [((i + 0.5) / len(stages), i, fn) for i, fn in enumerate(stages)]
    for _, _, fn in sorted(tagged, key=lambda t: t[:2]):
        fn()


class _Masks:
    def __init__(self, rows):
        c = CHUNK
        ri, ci = _iota2((c, c), 0), _iota2((c, c), 1)
        self.causal = ci <= ri
        bi, bj = _iota2((rows, rows), 0), _iota2((rows, rows), 1)
        chunk_of = lambda i: lax.shift_right_logical(i, c.bit_length() - 1)
        same_block = chunk_of(bi) == chunk_of(bj)
        tri = ((bj <= bi) & same_block).astype(BF16)
        self.tri2 = jnp.concatenate([tri, tri], axis=1)
        gi, gj = _iota2((2 * c, 4 * c), 0), _iota2((2 * c, 4 * c), 1)
        ti, sj = gi & (c - 1), gj & (c - 1)
        self.gmask = (sj < ti) | ((gi >= c) & (sj == ti))
        li, lj = _iota2((LANES, LANES), 0), _iota2((LANES, LANES), 1)
        head_of = lambda i: lax.shift_right_logical(i, HB_HEAD_DIM.bit_length() - 1)
        self.bdmask = head_of(li) == head_of(lj)
        self.lane_lo = _iota2((1, LANES), 1) < HB_HEAD_DIM
        self.row0 = (_iota2((rows, 1), 0) & (c - 1)) == 0
        self.lora_lane = _iota2((1, LORA_PAD), 1)


def _hgrn_stages(ctx, hs_ref, b0, nb, w_a, mk):
    hd = HA_HEAD_DIM
    chains = [(b, h) for b in range(nb) for h in range(w_a // hd)]
    st = {}

    def scores():
        qt, kt = ctx['hg_qt'], ctx['hg_kt']
        st['s_old'] = {ch: hs_ref[b0 + ch[0], ch[1]] for ch in chains}
        st['sc'] = {ch: jnp.where(mk.causal, _dot_nt(_tile(qt, *ch, hd), _tile(kt, *ch, hd)), 0.0).astype(BF16)
                    for ch in chains}

    def outputs():
        v, q0, kl, d = ctx['hg_v'], ctx['hg_q0'], ctx['hg_kl'], ctx['hg_d']
        s_old = st['s_old']
        o = {ch: _dot(st['sc'][ch], _tile(v, *ch, hd)) + _dot_nt(_tile(q0, *ch, hd), s_old[ch].astype(BF16))
             for ch in chains}
        for ch in chains:
            hs_ref[b0 + ch[0], ch[1]] = (_tile(d, *ch, hd)[0:1] * s_old[ch]
                                         + _dot_tn(_tile(v, *ch, hd), _tile(kl, *ch, hd)))
        st['o'] = o

    def normalise():
        o = {ch: t * lax.rsqrt(jnp.mean(t * t, axis=-1, keepdims=True) + NORM_EPS) for ch, t in st['o'].items()}
        ctx['o_a'] = _assemble(o, nb, w_a // hd)

    return [scores, outputs, normalise]


def _rwkv_stages(ctx, rs_ref, b0, nb, w_b, mk):
    c = CHUNK
    n_pairs = w_b // LANES
    lane_lo = mk.lane_lo
    pairs = [(b, pr) for b in range(nb) for pr in range(n_pairs)]
    tl = lambda t, pair: _tile(t, pair[0], pair[1], LANES)
    zero = jnp.zeros((c, LANES), BF16)
    lo = lambda t: jnp.where(lane_lo, t, zero)
    hi = lambda t: jnp.where(lane_lo, zero, t)
    cat0 = lambda *ts: jnp.concatenate(ts, axis=0)
    st = {}

    def state_read():
        at, rt, bt, kt, a0, r0, bl, kl, v = ctx['rw']
        st['s_old'] = {pp: rs_ref[b0 + pp[0], pp[1]] for pp in pairs}
        st['z'] = {pp: _dot_nt(cat0(tl(a0, pp), tl(r0, pp)), st['s_old'][pp].astype(BF16)) for pp in pairs}

    def gram():
        at, rt, bt, kt, a0, r0, bl, kl, v = ctx['rw']
        g = {}
        for pp in pairs:
            b_, k_ = tl(bt, pp), tl(kt, pp)
            g[pp] = jnp.where(
                mk.gmask, _dot_nt(cat0(tl(at, pp), tl(rt, pp)), cat0(lo(b_), hi(b_), lo(k_), hi(k_))), 0.0)
        st['g'] = g

    def rhs0():
        v = ctx['rw'][8]
        x, pw = {}, {}
        for pp in pairs:
            pw[pp] = st['g'][pp][:c, :LANES].astype(BF16)
            lak = st['g'][pp][:c, LANES:].astype(BF16)
            x[pp] = st['z'][pp][:c] + _dot(lak, cat0(lo(tl(v, pp)), hi(tl(v, pp))))
        st['x'], st['pw'] = x, pw

    def solve_step():
        x, pw, res = st['x'], st['pw'], {}
        for pp in pairs:
            xb = x[pp].astype(BF16)
            rhs = jnp.concatenate([cat0(lo(xb), hi(xb)), cat0(lo(pw[pp]), hi(pw[pp]))], axis=1)
            res[pp] = _dot(pw[pp], rhs)
        st['x'] = {pp: x[pp] + res[pp][:, :LANES] for pp in pairs}
        st['pw'] = {pp: res[pp][:, LANES:].astype(BF16) for pp in pairs}

    def solve_last():
        x, pw = st['x'], st['pw']
        for pp in pairs:
            xb = x[pp].astype(BF16)
            x[pp] = x[pp] + _dot(pw[pp], cat0(lo(xb), hi(xb)))

    def outputs():
        at, rt, bt, kt, a0, r0, bl, kl, v = ctx['rw']
        dl = ctx['rw_dl']
        y = {}
        for pp in pairs:
            ub, vv = st['x'][pp].astype(BF16), tl(v, pp)
            y[pp] = st['z'][pp][c:] + _dot(st['g'][pp][c:].astype(BF16), cat0(lo(ub), hi(ub), lo(vv), hi(vv)))
            upd = _dot_tn(cat0(ub, vv), cat0(tl(bl, pp), tl(kl, pp)))
            rs_ref[b0 + pp[0], pp[1]] = tl(dl, pp)[0:1] * st['s_old'][pp] + jnp.where(mk.bdmask, upd, 0.0)
        ctx['y'] = _assemble(y, nb, n_pairs)

    n_steps = 0
    n = 1
    while 2 * n < c:
        n_steps += 1
        n *= 2
    return [state_read, gram, rhs0] + [solve_step] * n_steps + [solve_last, outputs]


def _mixer_kernel(x_ref, xn_ref, mn_ref, win_f32, lbl_ref, hon_ref, mu_ref, w0_ref, a0_ref, wcomb_ref,
                  kk_ref, ka_ref, rk_ref, gnw_ref, gnb_ref, wout_f32, o_ref,
                  p_ref, hs_ref, rs_ref, carry_ref, win_s, wout_s, *, n_load, **dims):
    j = pl.program_id(1)

    @pl.when(j < n_load)
    def _():
        ri, ro = win_f32.shape[0], wout_f32.shape[0]
        n_in = win_f32.shape[1]
        rows_in = pl.ds(pl.multiple_of(j * ri, ri), ri)
        win_s[rows_in, 0:n_in] = win_f32[...].astype(BF16)
        win_s[rows_in, n_in:] = jnp.zeros((ri, win_s.shape[1] - n_in), BF16)
        wout_s[pl.ds(pl.multiple_of(j * ro, ro), ro), :] = wout_f32[...].astype(BF16)

    @pl.when(j >= n_load)
    def _():
        _mixer_step(x_ref, xn_ref, mn_ref, win_s, lbl_ref, hon_ref, mu_ref, w0_ref, a0_ref, wcomb_ref,
                    kk_ref, ka_ref, rk_ref, gnw_ref, gnb_ref, wout_s, o_ref,
                    p_ref, hs_ref, rs_ref, carry_ref, first_step=n_load, **dims)


def _mixer_step(x_ref, xn_ref, mn_ref, win_ref, lbl_ref, hon_ref, mu_ref, w0_ref, a0_ref, wcomb_ref,
                kk_ref, ka_ref, rk_ref, gnw_ref, gnb_ref, wout_ref, o_ref,
                p_ref, hs_ref, rs_ref, carry_ref, *, first_step, nb, n_groups, w_a, w_b):
    c = CHUNK
    rows = nb * c
    d_model = x_ref.shape[-1]
    n_rkv = 3 * w_b + LORA_PAD

    def in_proj_stages(grp, load_x):
        prow = slice(grp * rows, (grp + 1) * rows)
        hn = {}

        def norm():
            x = load_x().reshape(rows, d_model)
            hn['h'] = _rms_norm(x, mn_ref[...]).astype(BF16)

        def cols(lo, hi):
            def project():
                p_ref[prow, lo:hi] = _dot(hn['h'], win_ref[:, lo:hi])
            return project

        n_cols = 4 * w_a + n_rkv
        return [norm] + [cols(lo, min(lo + IN_PROJ_COL_TILE, n_cols))
                         for lo in range(0, n_cols, IN_PROJ_COL_TILE)]

    group_x = lambda grp: (lambda: x_ref[grp * nb:(grp + 1) * nb])

    @pl.when(pl.program_id(1) == first_step)
    def _():
        hs_ref[...] = jnp.zeros_like(hs_ref)
        rs_ref[...] = jnp.zeros_like(rs_ref)
        carry_ref[...] = jnp.zeros_like(carry_ref)
        _merge_stages(in_proj_stages(0, group_x(0)))

    mk = _Masks(c)
    logits = lbl_ref[...]
    e = jnp.exp(logits - jnp.max(logits, axis=0, keepdims=True))
    lb = e[0:1] / jnp.sum(e, axis=0, keepdims=True)

    def prep_stages(grp, ctx):
        bctxs = [dict() for _ in range(nb)]
        per_batch = [prep_batch_stages(grp * nb + b, bctxs[b]) for b in range(nb)]

        def collect():
            rows_of = lambda name: jnp.concatenate([bc_[name] for bc_ in bctxs], axis=0)
            for name in ('hg_qt', 'hg_kt', 'hg_kl', 'hg_q0', 'hg_d', 'hg_v', 'hg_gate', 'bonus', 'gate', 'rw_dl'):
                ctx[name] = rows_of(name)
            ctx['rw'] = tuple(jnp.concatenate([bc_['rw'][i] for bc_ in bctxs], axis=0) for i in range(9))

        return [fn for stage in zip(*per_batch) for fn in stage] + [collect]

    def prep_batch_stages(bidx, ctx):
        prow = slice(bidx * c, (bidx + 1) * c)

        def hgrn_gates():
            q_a = p_ref[prow, 0:w_a]
            f_a = p_ref[prow, w_a:2 * w_a]
            forget = lb + (1.0 - lb) * _sigmoid(f_a)
            ctx['hg_logf'] = _split_bf16(jnp.log(forget))
            ctx['hg_kh'] = 1.0 - forget
            ctx['hg_q'] = q_a * _sigmoid(q_a)
            g_a = p_ref[prow, 3 * w_a:4 * w_a]
            ctx['hg_gate'] = hon_ref[...] * (g_a * _sigmoid(g_a))

        def hgrn_decays():
            hi, lo = ctx['hg_logf']
            bc = _dot(mk.tri2, jnp.concatenate([hi, lo], axis=0))
            q, kh = ctx['hg_q'], ctx['hg_kh']
            b_ref = _block_rows(bc, c // 2)
            qt = q * jnp.exp(bc - b_ref)
            kt = kh * jnp.exp(b_ref - bc)
            ctx['hg_qt'] = qt.astype(BF16)
            ctx['hg_kt'] = kt.astype(BF16)
            ctx['hg_kl'] = (kt * _block_rows(bc, c - 1, lambda last, ref: jnp.exp(last - ref))).astype(BF16)
            ctx['hg_q0'] = (qt * _block_rows(bc, c - 1, lambda last, ref: jnp.exp(ref))).astype(BF16)
            ctx['hg_d'] = _block_rows(bc, c - 1, lambda last, ref: jnp.exp(last))
            ctx['hg_v'] = p_ref[prow, 2 * w_a:3 * w_a].astype(BF16)

        def rwkv_shift():
            pb = p_ref[prow, 4 * w_a:4 * w_a + n_rkv]
            prev = jnp.broadcast_to(carry_ref[bidx:bidx + 1, :], (c, n_rkv))
            shifted = jnp.where(mk.row0, prev, pltpu.roll(pb, 1, 0))
            carry_ref[bidx:bidx + 1, :] = pb[c - 1:c, :]
            pb = pb + mu_ref[...] * (shifted - pb)
            ctx['r'] = pb[:, 0:w_b]
            ctx['k'] = pb[:, w_b:2 * w_b]
            ctx['v'] = pb[:, 2 * w_b:3 * w_b]
            low = pb[:, 3 * w_b:]
            ll = mk.lora_lane
            act = jnp.where(ll < DECAY_LORA, jnp.tanh(low),
                            jnp.where(ll < DECAY_LORA + AAA_LORA, low,
                                      jnp.where(ll < DECAY_LORA + AAA_LORA + GATE_LORA, _sigmoid(low), 0.0)))
            ctx['act'] = act.astype(BF16)

        def rwkv_lora():
            r, k = ctx['r'], ctx['k']
            ld = _dot(ctx['act'], wcomb_ref[...])
            lw = -DECAY_SCALE * _sigmoid(w0_ref[...] + ld[:, 0:w_b])
            a = _sigmoid(a0_ref[...] + ld[:, w_b:2 * w_b])
            ctx['gate'] = ld[:, 2 * w_b:3 * w_b]
            kk = k * kk_ref[...]
            k = k * (1.0 + (a - 1.0) * ka_ref[...])
            ctx['k'], ctx['a'], ctx['kk'], ctx['lw'] = k, a, kk, lw
            ctx['lw_split'] = _split_bf16(lw)
            ctx['kk_sq'] = kk * kk
            ctx['rk'] = r * k * rk_ref[...]

        def rwkv_sums():
            hi, lo = ctx['lw_split']
            ctx['gc'] = _dot(mk.tri2, jnp.concatenate([hi, lo], axis=0))
            kk = ctx['kk'] * lax.rsqrt(jnp.maximum(_segsum(ctx['kk_sq'], mk.lane_lo), L2_EPS * L2_EPS))
            ctx['bonus'] = _segsum(ctx['rk'], mk.lane_lo) * ctx['v']
            ctx['av'], ctx['bv'] = -kk, kk * ctx['a']

        def rwkv_decays():
            r, k, v, av, bv, lw, gc = (ctx[n] for n in ('r', 'k', 'v', 'av', 'bv', 'lw', 'gc'))
            gref = _block_rows(gc, c // 2)
            e_up = jnp.exp(gc - gref)
            e_dn = jnp.exp(gref - gc)
            at, rt, bt, kt = av * jnp.exp(gc - lw - gref), r * e_up, bv * e_dn, k * e_dn
            s_ref = _block_rows(gc, c - 1, lambda last, ref: jnp.exp(ref))
            s_last = _block_rows(gc, c - 1, lambda last, ref: jnp.exp(last - ref))
            full = (at, rt, bt, kt, at * s_ref, rt * s_ref, bt * s_last, kt * s_last, v)
            ctx['rw'] = tuple(t.astype(BF16) for t in full)
            ctx['rw_dl'] = _block_rows(gc, c - 1, lambda last, ref: jnp.exp(last))

        return [hgrn_gates, hgrn_decays, rwkv_shift, rwkv_lora, rwkv_sums, rwkv_decays]

    def chain_stages(grp, ctx):
        b0 = grp * nb
        return _hgrn_stages(ctx, hs_ref, b0, nb, w_a, mk) + _rwkv_stages(ctx, rs_ref, b0, nb, w_b, mk)

    def post_stages(grp, ctx):
        b0 = grp * nb
        inv_n = 1.0 / HB_HEAD_DIM
        parts = [dict() for _ in range(nb)]

        def batch_stages(b, st):
            brow = slice(b * c, (b + 1) * c)

            def gate_a():
                st['o_a'] = (ctx['o_a'][brow] * ctx['hg_gate'][brow]).astype(BF16)
                st['y'] = ctx['y'][brow]

            def gn_mean():
                yc = st['y'] - _segsum(st['y'], mk.lane_lo) * inv_n
                st['yc'], st['yc_sq'] = yc, yc * yc

            def gn_var():
                var = _segsum(st['yc_sq'], mk.lane_lo) * inv_n
                yn = st['yc'] * lax.rsqrt(var + RWKV_GN_EPS) * gnw_ref[...] + gnb_ref[...]
                o_b = ((yn + ctx['bonus'][brow]) * ctx['gate'][brow]).astype(BF16)
                st['o'] = jnp.concatenate([st['o_a'], o_b], axis=1)

            return [gate_a, gn_mean, gn_var]

        def out_proj():
            o = jnp.concatenate([st['o'] for st in parts], axis=0)
            x = x_ref[b0:b0 + nb].reshape(rows, d_model)
            o_ref[b0:b0 + nb] = (x + _dot(o, wout_ref[...])).reshape(nb, c, d_model)

        per_batch = [batch_stages(b, parts[b]) for b in range(nb)]
        return [fn for stage in zip(*per_batch) for fn in stage] + [out_proj]

    ctxs = [dict() for _ in range(n_groups)]
    lists = [prep_stages(0, ctxs[0])]
    if n_groups > 1:
        lists.append(in_proj_stages(1, group_x(1)))
    _merge_stages(*lists)
    next_proj = in_proj_stages(0, lambda: xn_ref[...])
    n_early = 1 + (len(next_proj) - 1) // 2
    for grp in range(n_groups):
        lists = [chain_stages(grp, ctxs[grp])]
        if grp + 1 < n_groups:
            lists.append(prep_stages(grp + 1, ctxs[grp + 1]))
        if grp + 2 < n_groups:
            lists.append(in_proj_stages(grp + 2, group_x(grp + 2)))
        if grp >= 1:
            lists.append(post_stages(grp - 1, ctxs[grp - 1]))
        if grp == 0:
            lists.append(next_proj[:n_early])
        _merge_stages(*lists)
    _merge_stages(post_stages(n_groups - 1, ctxs[n_groups - 1]), next_proj[n_early:])


def _mixer(x, mix_norm, w_in, lb_logits, hgrn_out_norm, mu, w0, w2, a0, a2, g2, k_k, k_a, r_k,
           gn_w, gn_b, w_out, *, layer):
    bsz, t, d = x.shape
    w_a = hgrn_out_norm.shape[-1]
    w_b = w0.shape[-1]
    nb, n_groups, c = MIXER_GROUP, MIXER_GROUPS_PER_STEP, CHUNK
    nbs = nb * n_groups
    n_lora = DECAY_LORA + AAA_LORA + GATE_LORA
    assert bsz % nbs == 0 and t % c == 0
    assert w_a % HA_HEAD_DIM == 0 and w_b % LANES == 0
    assert w_in.shape[-1] == 4 * w_a + 3 * w_b + n_lora
    pad = LORA_PAD - n_lora
    n_cols = 4 * w_a + 3 * w_b + LORA_PAD
    n_rkv = 3 * w_b + LORA_PAD
    mu_p = jnp.pad(mu.reshape(1, -1), ((0, 0), (0, pad)))
    wcomb = jnp.zeros((LORA_PAD, 3 * w_b), F32)
    wcomb = wcomb.at[0:DECAY_LORA, 0:w_b].set(w2)
    wcomb = wcomb.at[DECAY_LORA:DECAY_LORA + AAA_LORA, w_b:2 * w_b].set(a2)
    wcomb = wcomb.at[DECAY_LORA + AAA_LORA:n_lora, 2 * w_b:3 * w_b].set(g2)
    row = lambda vec: vec.reshape(1, -1).astype(F32)
    n_pairs = w_b // LANES
    n_heads_a = w_a // HA_HEAD_DIM
    state_bytes = nbs * (n_heads_a * HA_HEAD_DIM * HA_HEAD_DIM + n_pairs * LANES * LANES) * 4
    weight_bytes = (d * n_cols + LORA_PAD * 3 * w_b + (w_a + w_b) * d) * 2
    n_load, nt = MIXER_WEIGHT_SLABS, t // c
    assert bsz == nbs and d % (16 * n_load) == 0 and (w_a + w_b) % (16 * n_load) == 0
    slab_bytes = (d * w_in.shape[-1] + (w_a + w_b) * d) * 4 // n_load
    vmem = (weight_bytes + 2 * slab_bytes + state_bytes + 3 * nbs * c * n_cols * 4 + 4 * nbs * c * d * 4
            + 12 * MIB)
    slab = lambda i, j: (jnp.minimum(j, n_load - 1), 0)
    chunk = lambda i, j: (i, jnp.maximum(j - n_load, 0), 0)
    return pl.pallas_call(
        functools.partial(_mixer_kernel, n_load=n_load, nb=nb, n_groups=n_groups, w_a=w_a, w_b=w_b),
        grid=(1, n_load + nt),
        in_specs=[
            pl.BlockSpec((nbs, c, d), chunk),
            pl.BlockSpec((nb, c, d), lambda i, j: (0, jnp.clip(j - n_load + 1, 0, nt - 1), 0)),
            _resident((1, d)),
            pl.BlockSpec((None, d // n_load, w_in.shape[-1]),
                         lambda i, j: (layer, jnp.minimum(j, n_load - 1), 0)),
            _resident(lb_logits.shape),
            _resident((1, w_a)),
            _resident((1, n_rkv)),
            _resident((1, w_b)),
            _resident((1, w_b)),
            _resident((LORA_PAD, 3 * w_b)),
            _resident((1, w_b)),
            _resident((1, w_b)),
            _resident((1, w_b)),
            _resident((1, w_b)),
            _resident((1, w_b)),
            pl.BlockSpec(((w_a + w_b) // n_load, d), slab),
        ],
        out_specs=pl.BlockSpec((nbs, c, d), chunk),
        out_shape=jax.ShapeDtypeStruct((bsz, t, d), F32),
        scratch_shapes=[
            pltpu.VMEM((nbs * c, n_cols), F32),
            pltpu.VMEM((nbs, n_heads_a, HA_HEAD_DIM, HA_HEAD_DIM), F32),
            pltpu.VMEM((nbs, n_pairs, LANES, LANES), F32),
            pltpu.VMEM((nbs, n_rkv), F32),
            pltpu.VMEM((d, n_cols), BF16),
            pltpu.VMEM((w_a + w_b, d), BF16),
        ],
        compiler_params=pltpu.CompilerParams(
            dimension_semantics=("arbitrary", "arbitrary"), vmem_limit_bytes=vmem),
        name="mixer",
    )(x, x, row(mix_norm), w_in, lb_logits.astype(F32), row(hgrn_out_norm), mu_p, row(w0), row(a0),
      wcomb.astype(BF16), row(k_k), row(k_a), row(r_k), row(gn_w), row(gn_b), w_out)


def kernel(x, ffn1_norm, ffn1_w_gate, ffn1_w_up, ffn1_w_down, mix_norm, w_in, hgrn_lb_logits, hgrn_out_norm, rwkv_shift_mu, rwkv_w0, rwkv_w2, rwkv_a0, rwkv_a2, rwkv_g2, rwkv_k_k, rwkv_k_a, rwkv_r_k, rwkv_gn_w, rwkv_gn_b, w_out, ffn2_norm, ffn2_w_gate, ffn2_w_up, ffn2_w_down, final_norm):
    bsz, t, d = x.shape
    depth = ffn1_norm.shape[0]
    assert depth == 1 and hgrn_lb_logits.shape[0] == depth + 1
    l = 0
    row = lambda vec: vec.reshape(1, -1).astype(F32)
    fg = row(final_norm)
    h = _ffn(x.reshape(bsz * t, d), row(ffn1_norm[l]), ffn1_w_gate[l], ffn1_w_up[l], ffn1_w_down[l],
             fg, final_norm=False)
    h = _mixer(h.reshape(bsz, t, d), mix_norm[l], w_in, hgrn_lb_logits, hgrn_out_norm[l],
               rwkv_shift_mu[l], rwkv_w0[l], rwkv_w2[l], rwkv_a0[l], rwkv_a2[l], rwkv_g2[l],
               rwkv_k_k[l], rwkv_k_a[l], rwkv_r_k[l], rwkv_gn_w[l], rwkv_gn_b[l], w_out[l], layer=l)
    h = _ffn(h.reshape(bsz * t, d), row(ffn2_norm[l]), ffn2_w_gate[l], ffn2_w_up[l], ffn2_w_down[l],
             fg, final_norm=True)
    return h.reshape(bsz, t, d)
```

```python
import functools
import math

import jax
import jax.numpy as jnp
from jax import lax
from jax.experimental import pallas as pl
from jax.experimental.pallas import tpu as pltpu

F32 = jnp.float32
BF16 = jnp.bfloat16

NORM_EPS = 1e-6
RWKV_GN_EPS = 64e-5
L2_EPS = 1e-12
DECAY_SCALE = math.exp(-0.5)

LANES = 128
MXU_COLS = 256
HA_HEAD_DIM = 128
HB_HEAD_DIM = 64
CHUNK = 64
DECAY_LORA, AAA_LORA, GATE_LORA = 32, 32, 96
LORA_PAD = 256
FFN_ROW_TILE = 1024
FFN_COL_TILE = 768
FFN_WEIGHT_SLABS = 8
MIXER_WEIGHT_SLABS = 8
MIXER_GROUP = 4
MIXER_GROUPS_PER_STEP = 2
IN_PROJ_COL_TILE = 512
MIB = 1024 * 1024


def _dot(a, b):
    return jnp.dot(a, b, preferred_element_type=F32)


def _dot_nt(a, b):
    return lax.dot_general(a, b, (((1,), (1,)), ((), ())), preferred_element_type=F32)


def _dot_tn(a, b):
    return lax.dot_general(a, b, (((0,), (0,)), ((), ())), preferred_element_type=F32)


def _split_bf16(x):
    hi = x.astype(BF16)
    lo = (x - hi.astype(F32)).astype(BF16)
    return hi, lo


def _rms_norm(x, g):
    return x * lax.rsqrt(jnp.mean(x * x, axis=-1, keepdims=True) + NORM_EPS) * g


def _sigmoid(x):
    return 0.5 * jnp.tanh(0.5 * x) + 0.5


def _ffn_kernel(x_ref, g_ref, wg_ref, wu_ref, wd_ref, fn_ref, o_ref, wg_s, wu_s, wd_s, *,
                col_tiles, final_norm, n_load):
    i = pl.program_id(0)

    @pl.when(i < n_load)
    def _():
        rg, rd = wg_ref.shape[0], wd_ref.shape[0]
        wg_s[pl.ds(pl.multiple_of(i * rg, rg), rg), :] = wg_ref[...].astype(BF16)
        wu_s[pl.ds(pl.multiple_of(i * rg, rg), rg), :] = wu_ref[...].astype(BF16)
        wd_s[pl.ds(pl.multiple_of(i * rd, rd), rd), :] = wd_ref[...].astype(BF16)

    @pl.when(i >= n_load)
    def _():
        x = x_ref[...]
        h = _rms_norm(x, g_ref[...]).astype(BF16)
        acc = jnp.zeros(x.shape, F32)
        for lo, hi in col_tiles:
            gate = _dot(h, wg_s[:, lo:hi])
            up = _dot(h, wu_s[:, lo:hi])
            act = (gate * _sigmoid(gate) * up).astype(BF16)
            acc = acc + _dot(act, wd_s[lo:hi, :])
        y = x + 0.5 * acc
        if final_norm:
            y = _rms_norm(y, fn_ref[...])
        o_ref[...] = y


def _resident(shape):
    return pl.BlockSpec(shape, lambda *_: (0,) * len(shape), pipeline_mode=pl.Buffered(1))


def _ffn(x2d, norm_g, w_gate, w_up, w_down, final_g, *, final_norm):
    n, d = x2d.shape
    d_ff = w_gate.shape[1]
    tm, n_load = FFN_ROW_TILE, FFN_WEIGHT_SLABS
    assert n % tm == 0 and d_ff % MXU_COLS == 0
    assert d % (16 * n_load) == 0 and d_ff % (16 * n_load) == 0
    col_tiles = tuple((lo, min(lo + FFN_COL_TILE, d_ff)) for lo in range(0, d_ff, FFN_COL_TILE))
    weight_bytes = 3 * d * d_ff * 2
    slab_bytes = 3 * d * d_ff * 4 // n_load
    tile_bytes = tm * d * 4
    vmem = weight_bytes + 2 * slab_bytes + 8 * tile_bytes + 3 * tm * FFN_COL_TILE * 4 + 8 * MIB
    slab = lambda i: (jnp.minimum(i, n_load - 1), 0)
    tile = lambda i: (jnp.maximum(i - n_load, 0), 0)
    return pl.pallas_call(
        functools.partial(_ffn_kernel, col_tiles=col_tiles, final_norm=final_norm, n_load=n_load),
        grid=(n_load + n // tm,),
        in_specs=[
            pl.BlockSpec((tm, d), tile),
            _resident((1, d)),
            pl.BlockSpec((d // n_load, d_ff), slab),
            pl.BlockSpec((d // n_load, d_ff), slab),
            pl.BlockSpec((d_ff // n_load, d), slab),
            _resident((1, d)),
        ],
        out_specs=pl.BlockSpec((tm, d), tile),
        out_shape=jax.ShapeDtypeStruct((n, d), F32),
        scratch_shapes=[
            pltpu.VMEM((d, d_ff), BF16),
            pltpu.VMEM((d, d_ff), BF16),
            pltpu.VMEM((d_ff, d), BF16),
        ],
        compiler_params=pltpu.CompilerParams(
            dimension_semantics=("arbitrary",), vmem_limit_bytes=vmem),
        name="ffn_final" if final_norm else "ffn",
    )(x2d, norm_g, w_gate, w_up, w_down, final_g)


def _iota2(shape, axis):
    return lax.broadcasted_iota(jnp.int32, shape, axis)


def _segsum(x, lane_lo):
    outs = []
    for j in range(x.shape[1] // LANES):
        t = x[:, j * LANES:(j + 1) * LANES]
        lo = jnp.sum(jnp.where(lane_lo, t, 0.0), axis=-1, keepdims=True)
        hi = jnp.sum(jnp.where(lane_lo, 0.0, t), axis=-1, keepdims=True)
        outs.append(jnp.where(lane_lo, lo, hi))
    return jnp.concatenate(outs, axis=1)


def _block_rows(x, idx, fn=None):
    c = CHUNK
    outs = []
    for r in range(0, x.shape[0], c):
        row = x[r + idx:r + idx + 1]
        if fn is not None:
            row = fn(row, x[r + c // 2:r + c // 2 + 1])
        outs.append(jnp.broadcast_to(row, (c, x.shape[1])))
    return jnp.concatenate(outs, axis=0)


def _tile(x, b, j, width):
    return x[b * CHUNK:(b + 1) * CHUNK, j * width:(j + 1) * width]


def _assemble(tiles, nb, nj):
    return jnp.concatenate(
        [jnp.concatenate([tiles[(b, j)] for j in range(nj)], axis=1) for b in range(nb)], axis=0)


def _merge_stages(*stage_lists):
    tagged = []
    for stages in stage_lists:
        tagged += [((i + 0.5) / len(stages), i, fn) for i, fn in enumerate(stages)]
    for _, _, fn in sorted(tagged, key=lambda t: t[:2]):
        fn()


class _Masks:
    def __init__(self, rows):
        c = CHUNK
        ri, ci = _iota2((c, c), 0), _iota2((c, c), 1)
        self.causal = ci <= ri
        bi, bj = _iota2((rows, rows), 0), _iota2((rows, rows), 1)
        chunk_of = lambda i: lax.shift_right_logical(i, c.bit_length() - 1)
        same_block = chunk_of(bi) == chunk_of(bj)
        tri = ((bj <= bi) & same_block).astype(BF16)
        self.tri2 = jnp.concatenate([tri, tri], axis=1)
        gi, gj = _iota2((2 * c, 4 * c), 0), _iota2((2 * c, 4 * c), 1)
        ti, sj = gi & (c - 1), gj & (c - 1)
        self.gmask = (sj < ti) | ((gi >= c) & (sj == ti))
        li, lj = _iota2((LANES, LANES), 0), _iota2((LANES, LANES), 1)
        head_of = lambda i: lax.shift_right_logical(i, HB_HEAD_DIM.bit_length() - 1)
        self.bdmask = head_of(li) == head_of(lj)
        self.lane_lo = _iota2((1, LANES), 1) < HB_HEAD_DIM
        self.row0 = (_iota2((rows, 1), 0) & (c - 1)) == 0
        self.lora_lane = _iota2((1, LORA_PAD), 1)


def _hgrn_stages(ctx, hs_ref, b0, nb, w_a, mk):
    hd = HA_HEAD_DIM
    chains = [(b, h) for b in range(nb) for h in range(w_a // hd)]
    st = {}

    def scores():
        qt, kt = ctx['hg_qt'], ctx['hg_kt']
        st['s_old'] = {ch: hs_ref[b0 + ch[0], ch[1]] for ch in chains}
        st['sc'] = {ch: jnp.where(mk.causal, _dot_nt(_tile(qt, *ch, hd), _tile(kt, *ch, hd)), 0.0).astype(BF16)
                    for ch in chains}

    def outputs():
        v, q0, kl, d = ctx['hg_v'], ctx['hg_q0'], ctx['hg_kl'], ctx['hg_d']
        s_old = st['s_old']
        o = {ch: _dot(st['sc'][ch], _tile(v, *ch, hd)) + _dot_nt(_tile(q0, *ch, hd), s_old[ch].astype(BF16))
             for ch in chains}
        for ch in chains:
            hs_ref[b0 + ch[0], ch[1]] = (_tile(d, *ch, hd)[0:1] * s_old[ch]
                                         + _dot_tn(_tile(v, *ch, hd), _tile(kl, *ch, hd)))
        st['o'] = o

    def normalise():
        o = {ch: t * lax.rsqrt(jnp.mean(t * t, axis=-1, keepdims=True) + NORM_EPS) for ch, t in st['o'].items()}
        ctx['o_a'] = _assemble(o, nb, w_a // hd)

    return [scores, outputs, normalise]


def _rwkv_stages(ctx, rs_ref, b0, nb, w_b, mk):
    c = CHUNK
    n_pairs = w_b // LANES
    lane_lo = mk.lane_lo
    pairs = [(b, pr) for b in range(nb) for pr in range(n_pairs)]
    tl = lambda t, pair: _tile(t, pair[0], pair[1], LANES)
    zero = jnp.zeros((c, LANES), BF16)
    lo = lambda t: jnp.where(lane_lo, t, zero)
    hi = lambda t: jnp.where(lane_lo, zero, t)
    cat0 = lambda *ts: jnp.concatenate(ts, axis=0)
    st = {}

    def state_read():
        at, rt, bt, kt, a0, r0, bl, kl, v = ctx['rw']
        st['s_old'] = {pp: rs_ref[b0 + pp[0], pp[1]] for pp in pairs}
        st['z'] = {pp: _dot_nt(cat0(tl(a0, pp), tl(r0, pp)), st['s_old'][pp].astype(BF16)) for pp in pairs}

    def gram():
        at, rt, bt, kt, a0, r0, bl, kl, v = ctx['rw']
        g = {}
        for pp in pairs:
            b_, k_ = tl(bt, pp), tl(kt, pp)
            g[pp] = jnp.where(
                mk.gmask, _dot_nt(cat0(tl(at, pp), tl(rt, pp)), cat0(lo(b_), hi(b_), lo(k_), hi(k_))), 0.0)
        st['g'] = g

    def rhs0():
        v = ctx['rw'][8]
        x, pw = {}, {}
        for pp in pairs:
            pw[pp] = st['g'][pp][:c, :LANES].astype(BF16)
            lak = st['g'][pp][:c, LANES:].astype(BF16)
            x[pp] = st['z'][pp][:c] + _dot(lak, cat0(lo(tl(v, pp)), hi(tl(v, pp))))
        st['x'], st['pw'] = x, pw

    def solve_step():
        x, pw, res = st['x'], st['pw'], {}
        for pp in pairs:
            xb = x[pp].astype(BF16)
            rhs = jnp.concatenate([cat0(lo(xb), hi(xb)), cat0(lo(pw[pp]), hi(pw[pp]))], axis=1)
            res[pp] = _dot(pw[pp], rhs)
        st['x'] = {pp: x[pp] + res[pp][:, :LANES] for pp in pairs}
        st['pw'] = {pp: res[pp][:, LANES:].astype(BF16) for pp in pairs}

    def solve_last():
        x, pw = st['x'], st['pw']
        for pp in pairs:
            xb = x[pp].astype(BF16)
            x[pp] = x[pp] + _dot(pw[pp], cat0(lo(xb), hi(xb)))

    def outputs():
        at, rt, bt, kt, a0, r0, bl, kl, v = ctx['rw']
        dl = ctx['rw_dl']
        y = {}
        for pp in pairs:
            ub, vv = st['x'][pp].astype(BF16), tl(v, pp)
            y[pp] = st['z'][pp][c:] + _dot(st['g'][pp][c:].astype(BF16), cat0(lo(ub), hi(ub), lo(vv), hi(vv)))
            upd = _dot_tn(cat0(ub, vv), cat0(tl(bl, pp), tl(kl, pp)))
            rs_ref[b0 + pp[0], pp[1]] = tl(dl, pp)[0:1] * st['s_old'][pp] + jnp.where(mk.bdmask, upd, 0.0)
        ctx['y'] = _assemble(y, nb, n_pairs)

    n_steps = 0
    n = 1
    while 2 * n < c:
        n_steps += 1
        n *= 2
    return [state_read, gram, rhs0] + [solve_step] * n_steps + [solve_last, outputs]


def _mixer_kernel(x_ref, xn_ref, mn_ref, win_f32, lbl_ref, hon_ref, mu_ref, w0_ref, a0_ref, wcomb_ref,
                  kk_ref, ka_ref, rk_ref, gnw_ref, gnb_ref, wout_f32, o_ref,
                  p_ref, hs_ref, rs_ref, carry_ref, win_s, wout_s, *, n_load, **dims):
    j = pl.program_id(1)

    @pl.when(j < n_load)
    def _():
        ri, ro = win_f32.shape[0], wout_f32.shape[0]
        n_in = win_f32.shape[1]
        rows_in = pl.ds(pl.multiple_of(j * ri, ri), ri)
        win_s[rows_in, 0:n_in] = win_f32[...].astype(BF16)
        win_s[rows_in, n_in:] = jnp.zeros((ri, win_s.shape[1] - n_in), BF16)
        wout_s[pl.ds(pl.multiple_of(j * ro, ro), ro), :] = wout_f32[...].astype(BF16)

    @pl.when(j >= n_load)
    def _():
        _mixer_step(x_ref, xn_ref, mn_ref, win_s, lbl_ref, hon_ref, mu_ref, w0_ref, a0_ref, wcomb_ref,
                    kk_ref, ka_ref, rk_ref, gnw_ref, gnb_ref, wout_s, o_ref,
                    p_ref, hs_ref, rs_ref, carry_ref, first_step=n_load, **dims)


def _mixer_step(x_ref, xn_ref, mn_ref, win_ref, lbl_ref, hon_ref, mu_ref, w0_ref, a0_ref, wcomb_ref,
                kk_ref, ka_ref, rk_ref, gnw_ref, gnb_ref, wout_ref, o_ref,
                p_ref, hs_ref, rs_ref, carry_ref, *, first_step, nb, n_groups, w_a, w_b):
    c = CHUNK
    rows = nb * c
    d_model = x_ref.shape[-1]
    n_rkv = 3 * w_b + LORA_PAD

    def in_proj_stages(grp, load_x):
        prow = slice(grp * rows, (grp + 1) * rows)
        hn = {}

        def norm():
            x = load_x().reshape(rows, d_model)
            hn['h'] = _rms_norm(x, mn_ref[...]).astype(BF16)

        def cols(lo, hi):
            def project():
                p_ref[prow, lo:hi] = _dot(hn['h'], win_ref[:, lo:hi])
            return project

        n_cols = 4 * w_a + n_rkv
        return [norm] + [cols(lo, min(lo + IN_PROJ_COL_TILE, n_cols))
                         for lo in range(0, n_cols, IN_PROJ_COL_TILE)]

    group_x = lambda grp, half: (lambda: x_ref[grp * nb:(grp + 1) * nb, half * c:(half + 1) * c])

    @pl.when(pl.program_id(1) == first_step)
    def _():
        hs_ref[...] = jnp.zeros_like(hs_ref)
        rs_ref[...] = jnp.zeros_like(rs_ref)
        carry_ref[...] = jnp.zeros_like(carry_ref)
        _merge_stages(in_proj_stages(0, group_x(0, 0)))

    mk = _Masks(c)
    logits = lbl_ref[...]
    e = jnp.exp(logits - jnp.max(logits, axis=0, keepdims=True))
    lb = e[0:1] / jnp.sum(e, axis=0, keepdims=True)

    def prep_stages(grp, ctx):
        bctxs = [dict() for _ in range(nb)]
        per_batch = [prep_batch_stages(grp * nb + b, bctxs[b]) for b in range(nb)]

        def collect():
            rows_of = lambda name: jnp.concatenate([bc_[name] for bc_ in bctxs], axis=0)
            for name in ('hg_qt', 'hg_kt', 'hg_kl', 'hg_q0', 'hg_d', 'hg_v', 'hg_gate', 'bonus', 'gate', 'rw_dl'):
                ctx[name] = rows_of(name)
            ctx['rw'] = tuple(jnp.concatenate([bc_['rw'][i] for bc_ in bctxs], axis=0) for i in range(9))

        return [fn for stage in zip(*per_batch) for fn in stage] + [collect]

    def prep_batch_stages(bidx, ctx):
        prow = slice(bidx * c, (bidx + 1) * c)

        def hgrn_gates():
            q_a = p_ref[prow, 0:w_a]
            f_a = p_ref[prow, w_a:2 * w_a]
            forget = lb + (1.0 - lb) * _sigmoid(f_a)
            ctx['hg_logf'] = _split_bf16(jnp.log(forget))
            ctx['hg_kh'] = 1.0 - forget
            ctx['hg_q'] = q_a * _sigmoid(q_a)
            g_a = p_ref[prow, 3 * w_a:4 * w_a]
            ctx['hg_gate'] = hon_ref[...] * (g_a * _sigmoid(g_a))

        def hgrn_decays():
            hi, lo = ctx['hg_logf']
            bc = _dot(mk.tri2, jnp.concatenate([hi, lo], axis=0))
            q, kh = ctx['hg_q'], ctx['hg_kh']
            b_ref = _block_rows(bc, c // 2)
            qt = q * jnp.exp(bc - b_ref)
            kt = kh * jnp.exp(b_ref - bc)
            ctx['hg_qt'] = qt.astype(BF16)
            ctx['hg_kt'] = kt.astype(BF16)
            ctx['hg_kl'] = (kt * _block_rows(bc, c - 1, lambda last, ref: jnp.exp(last - ref))).astype(BF16)
            ctx['hg_q0'] = (qt * _block_rows(bc, c - 1, lambda last, ref: jnp.exp(ref))).astype(BF16)
            ctx['hg_d'] = _block_rows(bc, c - 1, lambda last, ref: jnp.exp(last))
            ctx['hg_v'] = p_ref[prow, 2 * w_a:3 * w_a].astype(BF16)

        def rwkv_shift():
            pb = p_ref[prow, 4 * w_a:4 * w_a + n_rkv]
            prev = jnp.broadcast_to(carry_ref[bidx:bidx + 1, :], (c, n_rkv))
            shifted = jnp.where(mk.row0, prev, pltpu.roll(pb, 1, 0))
            carry_ref[bidx:bidx + 1, :] = pb[c - 1:c, :]
            pb = pb + mu_ref[...] * (shifted - pb)
            ctx['r'] = pb[:, 0:w_b]
            ctx['k'] = pb[:, w_b:2 * w_b]
            ctx['v'] = pb[:, 2 * w_b:3 * w_b]
            low = pb[:, 3 * w_b:]
            ll = mk.lora_lane
            act = jnp.where(ll < DECAY_LORA, jnp.tanh(low),
                            jnp.where(ll < DECAY_LORA + AAA_LORA, low,
                                      jnp.where(ll < DECAY_LORA + AAA_LORA + GATE_LORA, _sigmoid(low), 0.0)))
            ctx['act'] = act.astype(BF16)

        def rwkv_lora():
            r, k = ctx['r'], ctx['k']
            ld = _dot(ctx['act'], wcomb_ref[...])
            lw = -DECAY_SCALE * _sigmoid(w0_ref[...] + ld[:, 0:w_b])
            a = _sigmoid(a0_ref[...] + ld[:, w_b:2 * w_b])
            ctx['gate'] = ld[:, 2 * w_b:3 * w_b]
            kk = k * kk_ref[...]
            k = k * (1.0 + (a - 1.0) * ka_ref[...])
            ctx['k'], ctx['a'], ctx['kk'], ctx['lw'] = k, a, kk, lw
            ctx['lw_split'] = _split_bf16(lw)
            ctx['kk_sq'] = kk * kk
            ctx['rk'] = r * k * rk_ref[...]

        def rwkv_sums():
            hi, lo = ctx['lw_split']
            ctx['gc'] = _dot(mk.tri2, jnp.concatenate([hi, lo], axis=0))
            kk = ctx['kk'] * lax.rsqrt(jnp.maximum(_segsum(ctx['kk_sq'], mk.lane_lo), L2_EPS * L2_EPS))
            ctx['bonus'] = _segsum(ctx['rk'], mk.lane_lo) * ctx['v']
            ctx['av'], ctx['bv'] = -kk, kk * ctx['a']

        def rwkv_decays():
            r, k, v, av, bv, lw, gc = (ctx[n] for n in ('r', 'k', 'v', 'av', 'bv', 'lw', 'gc'))
            gref = _block_rows(gc, c // 2)
            e_up = jnp.exp(gc - gref)
            e_dn = jnp.exp(gref - gc)
            at, rt, bt, kt = av * jnp.exp(gc - lw - gref), r * e_up, bv * e_dn, k * e_dn
            s_ref = _block_rows(gc, c - 1, lambda last, ref: jnp.exp(ref))
            s_last = _block_rows(gc, c - 1, lambda last, ref: jnp.exp(last - ref))
            full = (at, rt, bt, kt, at * s_ref, rt * s_ref, bt * s_last, kt * s_last, v)
            ctx['rw'] = tuple(t.astype(BF16) for t in full)
            ctx['rw_dl'] = _block_rows(gc, c - 1, lambda last, ref: jnp.exp(last))

        return [hgrn_gates, hgrn_decays, rwkv_shift, rwkv_lora, rwkv_sums, rwkv_decays]

    def chain_stages(grp, ctx):
        b0 = grp * nb
        return _hgrn_stages(ctx, hs_ref, b0, nb, w_a, mk) + _rwkv_stages(ctx, rs_ref, b0, nb, w_b, mk)

    def post_stages(grp, half, ctx):
        b0 = grp * nb
        trow = slice(half * c, (half + 1) * c)
        inv_n = 1.0 / HB_HEAD_DIM
        parts = [dict() for _ in range(nb)]

        def batch_stages(b, st):
            brow = slice(b * c, (b + 1) * c)

            def gate_a():
                st['o_a'] = (ctx['o_a'][brow] * ctx['hg_gate'][brow]).astype(BF16)
                st['y'] = ctx['y'][brow]

            def gn_mean():
                yc = st['y'] - _segsum(st['y'], mk.lane_lo) * inv_n
                st['yc'], st['yc_sq'] = yc, yc * yc

            def gn_var():
                var = _segsum(st['yc_sq'], mk.lane_lo) * inv_n
                yn = st['yc'] * lax.rsqrt(var + RWKV_GN_EPS) * gnw_ref[...] + gnb_ref[...]
                o_b = ((yn + ctx['bonus'][brow]) * ctx['gate'][brow]).astype(BF16)
                st['o'] = jnp.concatenate([st['o_a'], o_b], axis=1)

            return [gate_a, gn_mean, gn_var]

        def out_proj():
            o = jnp.concatenate([st['o'] for st in parts], axis=0)
            x = x_ref[b0:b0 + nb, trow].reshape(rows, d_model)
            o_ref[b0:b0 + nb, trow] = (x + _dot(o, wout_ref[...])).reshape(nb, c, d_model)

        per_batch = [batch_stages(b, parts[b]) for b in range(nb)]
        return [fn for stage in zip(*per_batch) for fn in stage] + [out_proj]

    assert n_groups == 2 and x_ref.shape[1] == 2 * c
    ctx = {(g, h): dict() for g in range(2) for h in range(2)}
    mid_proj = in_proj_stages(0, group_x(0, 1))
    next_proj = in_proj_stages(0, lambda: xn_ref[...])
    n_early = 1 + (len(next_proj) - 1) // 2
    _merge_stages(prep_stages(0, ctx[0, 0]), in_proj_stages(1, group_x(1, 0)))
    _merge_stages(chain_stages(0, ctx[0, 0]), prep_stages(1, ctx[1, 0]), mid_proj[:n_early])
    _merge_stages(chain_stages(1, ctx[1, 0]), post_stages(0, 0, ctx[0, 0]), mid_proj[n_early:])
    _merge_stages(post_stages(1, 0, ctx[1, 0]), prep_stages(0, ctx[0, 1]), in_proj_stages(1, group_x(1, 1)))
    _merge_stages(chain_stages(0, ctx[0, 1]), prep_stages(1, ctx[1, 1]), next_proj[:n_early])
    _merge_stages(chain_stages(1, ctx[1, 1]), post_stages(0, 1, ctx[0, 1]))
    _merge_stages(post_stages(1, 1, ctx[1, 1]), next_proj[n_early:])


def _mixer(x, mix_norm, w_in, lb_logits, hgrn_out_norm, mu, w0, w2, a0, a2, g2, k_k, k_a, r_k,
           gn_w, gn_b, w_out, *, layer):
    bsz, t, d = x.shape
    w_a = hgrn_out_norm.shape[-1]
    w_b = w0.shape[-1]
    nb, n_groups, c = MIXER_GROUP, MIXER_GROUPS_PER_STEP, CHUNK
    nbs = nb * n_groups
    n_lora = DECAY_LORA + AAA_LORA + GATE_LORA
    assert bsz % nbs == 0 and t % c == 0
    assert w_a % HA_HEAD_DIM == 0 and w_b % LANES == 0
    assert w_in.shape[-1] == 4 * w_a + 3 * w_b + n_lora
    pad = LORA_PAD - n_lora
    n_cols = 4 * w_a + 3 * w_b + LORA_PAD
    n_rkv = 3 * w_b + LORA_PAD
    mu_p = jnp.pad(mu.reshape(1, -1), ((0, 0), (0, pad)))
    wcomb = jnp.zeros((LORA_PAD, 3 * w_b), F32)
    wcomb = wcomb.at[0:DECAY_LORA, 0:w_b].set(w2)
    wcomb = wcomb.at[DECAY_LORA:DECAY_LORA + AAA_LORA, w_b:2 * w_b].set(a2)
    wcomb = wcomb.at[DECAY_LORA + AAA_LORA:n_lora, 2 * w_b:3 * w_b].set(g2)
    row = lambda vec: vec.reshape(1, -1).astype(F32)
    n_pairs = w_b // LANES
    n_heads_a = w_a // HA_HEAD_DIM
    state_bytes = nbs * (n_heads_a * HA_HEAD_DIM * HA_HEAD_DIM + n_pairs * LANES * LANES) * 4
    weight_bytes = (d * n_cols + LORA_PAD * 3 * w_b + (w_a + w_b) * d) * 2
    n_load, nt = MIXER_WEIGHT_SLABS, t // c
    assert bsz == nbs and d % (16 * n_load) == 0 and (w_a + w_b) % (16 * n_load) == 0
    assert nt % 2 == 0
    slab_bytes = (d * w_in.shape[-1] + (w_a + w_b) * d) * 4 // n_load
    vmem = (weight_bytes + 2 * slab_bytes + state_bytes + 2 * nbs * c * n_cols * 4 + 9 * nbs * c * d * 4
            + 12 * MIB)
    slab = lambda i, j: (jnp.minimum(j, n_load - 1), 0)
    chunk = lambda i, j: (i, jnp.maximum(j - n_load, 0), 0)
    return pl.pallas_call(
        functools.partial(_mixer_kernel, n_load=n_load, nb=nb, n_groups=n_groups, w_a=w_a, w_b=w_b),
        grid=(1, n_load + nt // 2),
        in_specs=[
            pl.BlockSpec((nbs, 2 * c, d), chunk),
            pl.BlockSpec((nb, c, d), lambda i, j: (0, jnp.clip(2 * (j - n_load + 1), 0, nt - 1), 0)),
            _resident((1, d)),
            pl.BlockSpec((None, d // n_load, w_in.shape[-1]),
                         lambda i, j: (layer, jnp.minimum(j, n_load - 1), 0)),
            _resident(lb_logits.shape),
            _resident((1, w_a)),
            _resident((1, n_rkv)),
            _resident((1, w_b)),
            _resident((1, w_b)),
            _resident((LORA_PAD, 3 * w_b)),
            _resident((1, w_b)),
            _resident((1, w_b)),
            _resident((1, w_b)),
            _resident((1, w_b)),
            _resident((1, w_b)),
            pl.BlockSpec(((w_a + w_b) // n_load, d), slab),
        ],
        out_specs=pl.BlockSpec((nbs, 2 * c, d), chunk),
        out_shape=jax.ShapeDtypeStruct((bsz, t, d), F32),
        scratch_shapes=[
            pltpu.VMEM((nbs * c, n_cols), F32),
            pltpu.VMEM((nbs, n_heads_a, HA_HEAD_DIM, HA_HEAD_DIM), F32),
            pltpu.VMEM((nbs, n_pairs, LANES, LANES), F32),
            pltpu.VMEM((nbs, n_rkv), F32),
            pltpu.VMEM((d, n_cols), BF16),
            pltpu.VMEM((w_a + w_b, d), BF16),
        ],
        compiler_params=pltpu.CompilerParams(
            dimension_semantics=("arbitrary", "arbitrary"), vmem_limit_bytes=vmem),
        name="mixer",
    )(x, x, row(mix_norm), w_in, lb_logits.astype(F32), row(hgrn_out_norm), mu_p, row(w0), row(a0),
      wcomb.astype(BF16), row(k_k), row(k_a), row(r_k), row(gn_w), row(gn_b), w_out)


def kernel(x, ffn1_norm, ffn1_w_gate, ffn1_w_up, ffn1_w_down, mix_norm, w_in, hgrn_lb_logits, hgrn_out_norm, rwkv_shift_mu, rwkv_w0, rwkv_w2, rwkv_a0, rwkv_a2, rwkv_g2, rwkv_k_k, rwkv_k_a, rwkv_r_k, rwkv_gn_w, rwkv_gn_b, w_out, ffn2_norm, ffn2_w_gate, ffn2_w_up, ffn2_w_down, final_norm):
    bsz, t, d = x.shape
    depth = ffn1_norm.shape[0]
    assert depth == 1 and hgrn_lb_logits.shape[0] == depth + 1
    l = 0
    row = lambda vec: vec.reshape(1, -1).astype(F32)
    fg = row(final_norm)
    h = _ffn(x.reshape(bsz * t, d), row(ffn1_norm[l]), ffn1_w_gate[l], ffn1_w_up[l], ffn1_w_down[l],
             fg, final_norm=False)
    h = _mixer(h.reshape(bsz, t, d), mix_norm[l], w_in, hgrn_lb_logits, hgrn_out_norm[l],
               rwkv_shift_mu[l], rwkv_w0[l], rwkv_w2[l], rwkv_a0[l], rwkv_a2[l], rwkv_g2[l],
               rwkv_k_k[l], rwkv_k_a[l], rwkv_r_k[l], rwkv_gn_w[l], rwkv_gn_b[l], w_out[l], layer=l)
    h = _ffn(h.reshape(bsz * t, d), row(ffn2_norm[l]), ffn2_w_gate[l], ffn2_w_up[l], ffn2_w_down[l],
             fg, final_norm=True)
    return h.reshape(bsz, t, d)
```
